```python
import math
import jax, jax.numpy as jnp
from jax import lax
import numpy as np

D_MODEL = 1024
BATCH = 16
SEQ = 2048
DEPTH = 2
DEC_BATCH = 128
DEC_SEQ = 4
PAST_LEN = 16384
PAGE_SIZE = 128

N_META = 16
N_EVEN = (DEPTH + 1) // 2
N_ODD = DEPTH // 2
POOL_WINDOWS = (2, 4, 8, 16)
POOL_GROUPS = 4
POOL_WIDTH = D_MODEL // 2
POOL_GW = POOL_WIDTH // POOL_GROUPS
POOL_BUF = max(POOL_WINDOWS) - 1
N_HEADS = 8
QK_NOPE = 64
QK_ROPE = 32
V_HEAD = 64
Q_LORA = 384
KV_LORA = 256
ROPE_THETA = 10000.0
Q_BLOCK = 128
ATTN_WIDTH = N_HEADS * V_HEAD
ATTN_SCALE = (QK_NOPE + QK_ROPE) ** -0.5
IN_WIDTH_EVEN = POOL_WIDTH + Q_LORA + KV_LORA + QK_ROPE
MIX_WIDTH_EVEN = POOL_WIDTH + ATTN_WIDTH
CONV_CH = D_MODEL
CONV_W = 31
D_FF = 2816
N_EXPERTS = 8
TOP_K = 2
EXPERT_FF = 1408
EPS = 1e-6
NEG_INF = -1e30

kernel_name = 'hybrid_pool_mla_conformer_moe_step'


def rmsnorm(x, g):
    x32 = x.astype(jnp.float32)
    y = x32 * lax.rsqrt(jnp.mean(x32 * x32, axis=-1, keepdims=True) + EPS)
    return (y * g.astype(jnp.float32)).astype(x.dtype)


def layernorm(x, g, b):
    x32 = x.astype(jnp.float32)
    xc = x32 - jnp.mean(x32, axis=-1, keepdims=True)
    var = jnp.mean(xc * xc, axis=-1, keepdims=True)
    y = xc * lax.rsqrt(var + EPS) * g.astype(jnp.float32) + b.astype(jnp.float32)
    return y.astype(x.dtype)


def rope_tables(pos):
    inv = ROPE_THETA ** (-jnp.arange(0, QK_ROPE, 2, dtype=jnp.float32) / QK_ROPE)
    ang = pos[:, None] * inv[None, :]
    return jnp.cos(ang), jnp.sin(ang)


def apply_rope(x, cos, sin):
    half = QK_ROPE // 2
    shape = (cos.shape[0],) + (1,) * (x.ndim - 3) + (half,)
    c = cos.reshape(shape)
    s = sin.reshape(shape)
    x32 = x.astype(jnp.float32)
    x1, x2 = x32[..., :half], x32[..., half:]
    return jnp.concatenate([x1 * c - x2 * s, x2 * c + x1 * s], axis=-1).astype(x.dtype)


def swiglu(x, w_gu, w_down):
    g, u = jnp.split(x @ w_gu, 2, axis=-1)
    return (jax.nn.silu(g) * u) @ w_down


def moe_swiglu(x, router_w, w_gu, w_down):
    shp = x.shape
    xf = x.reshape(-1, shp[-1])
    logits = (xf @ router_w).astype(jnp.float32)
    top_v, top_i = lax.top_k(logits, TOP_K)
    gates = jax.nn.softmax(top_v, axis=-1)
    dense_gate = jnp.sum(jax.nn.one_hot(top_i, N_EXPERTS, dtype=jnp.float32) * gates[..., None], axis=1)
    y = jnp.zeros_like(xf)
    for e in range(N_EXPERTS):
        y = y + dense_gate[:, e:e + 1].astype(xf.dtype) * swiglu(xf, w_gu[e], w_down[e])
    return y.reshape(shp)


def pool_mix(u, buf, pos, w_grp, scale):
    B, S, C = u.shape
    L = POOL_BUF
    if buf is None:
        buf = jnp.zeros((B, L, C), u.dtype)
    z = jnp.concatenate([buf.astype(u.dtype), u], axis=1)
    csum = jnp.cumsum(z.astype(jnp.float32), axis=1)
    csum = jnp.concatenate([jnp.zeros((B, 1, C), jnp.float32), csum], axis=1)
    hi = csum[:, L + 1:L + 1 + S]
    means = []
    for g, w in enumerate(POOL_WINDOWS):
        sl = slice(g * POOL_GW, (g + 1) * POOL_GW)
        lo = csum[:, L + 1 - w:L + 1 - w + S, sl]
        cnt = jnp.minimum(pos + 1.0, float(w))
        means.append((hi[..., sl] - lo) / cnt[None, :, None])
    mean = jnp.stack(means, axis=2)
    d = (mean - u.astype(jnp.float32).reshape(B, S, POOL_GROUPS, POOL_GW)).astype(u.dtype)
    out = jnp.einsum('bsgc,gcd->bsgd', d, w_grp).reshape(B, S, C) * scale
    return out, z[:, -L:]


def mla_prompt_attn(q_nope, q_rope, c_kv, k_r, w_uk, w_uv):
    B, S = q_nope.shape[:2]
    k_nope = jnp.einsum('bsc,chd->bshd', c_kv, w_uk)
    v = jnp.einsum('bsc,chd->bshd', c_kv, w_uv)
    nblk = -(-S // Q_BLOCK)
    pad = nblk * Q_BLOCK - S

    def blocks(q):
        q = jnp.pad(q, ((0, 0), (0, pad), (0, 0), (0, 0)))
        return q.reshape((B, nblk, Q_BLOCK) + q.shape[2:]).swapaxes(0, 1)

    kpos = jnp.arange(S)

    def attend(args):
        qn, qr, start = args
        s = (jnp.einsum('bqhd,bkhd->bhqk', qn, k_nope)
             + jnp.einsum('bqhr,bkr->bhqk', qr, k_r)).astype(jnp.float32) * ATTN_SCALE
        qpos = start + jnp.arange(Q_BLOCK)
        s = jnp.where(kpos[None, :] <= qpos[:, None], s, NEG_INF)
        p = jax.nn.softmax(s, axis=-1).astype(v.dtype)
        return jnp.einsum('bhqk,bkhd->bqhd', p, v)

    starts = jnp.arange(nblk, dtype=jnp.int32) * Q_BLOCK
    o = lax.map(attend, (blocks(q_nope), blocks(q_rope), starts))
    return o.swapaxes(0, 1).reshape(B, nblk * Q_BLOCK, ATTN_WIDTH)[:, :S]


def mla_sample_attn(q_nope, q_rope, c_kv, k_r, past, w_uk, w_uv):
    Bd, Sd = q_nope.shape[:2]
    P = past.shape[1]
    q_lat = jnp.einsum('bshd,chd->bshc', q_nope, w_uk)
    q_cat = jnp.concatenate([q_lat, q_rope], axis=-1)
    new = jnp.concatenate([c_kv, k_r], axis=-1)
    s_past = jnp.einsum('bshc,bkc->bhsk', q_cat, past).astype(jnp.float32) * ATTN_SCALE
    s_new = jnp.einsum('bshc,btc->bhst', q_cat, new).astype(jnp.float32) * ATTN_SCALE
    causal = jnp.arange(Sd)[None, :] <= jnp.arange(Sd)[:, None]
    s_new = jnp.where(causal, s_new, NEG_INF)
    p = jax.nn.softmax(jnp.concatenate([s_past, s_new], axis=-1), axis=-1).astype(past.dtype)
    o_lat = (jnp.einsum('bhsk,bkc->bshc', p[..., :P], past)[..., :KV_LORA]
             + jnp.einsum('bhst,btc->bshc', p[..., P:], c_kv))
    return jnp.einsum('bshc,chd->bshd', o_lat, w_uv).reshape(Bd, Sd, ATTN_WIDTH)


def even_layer(h, pos, pool_buf, past, norm_mix, w_in, pool_w, pool_scale, q_norm, kv_norm,
               w_uq, w_uk, w_uv, w_out, norm_ffn, w_gu, w_down):
    B, S, _ = h.shape
    a = rmsnorm(h, norm_mix) @ w_in
    i1 = POOL_WIDTH
    i2 = i1 + Q_LORA
    i3 = i2 + KV_LORA
    u_pool, cq, ckv, kr = a[..., :i1], a[..., i1:i2], a[..., i2:i3], a[..., i3:]
    cos, sin = rope_tables(pos)
    q = (rmsnorm(cq, q_norm) @ w_uq).reshape(B, S, N_HEADS, QK_NOPE + QK_ROPE)
    q_nope = q[..., :QK_NOPE]
    q_rope = apply_rope(q[..., QK_NOPE:], cos, sin)
    c_kv = rmsnorm(ckv, kv_norm)
    k_r = apply_rope(kr, cos, sin)
    pool_out, new_pool = pool_mix(u_pool, pool_buf, pos, pool_w, pool_scale)
    if past is None:
        attn = mla_prompt_attn(q_nope, q_rope, c_kv, k_r, w_uk, w_uv)
    else:
        attn = mla_sample_attn(q_nope, q_rope, c_kv, k_r, past, w_uk, w_uv)
    new_kv = jnp.concatenate([c_kv, k_r], axis=-1)
    h = h + jnp.concatenate([pool_out, attn], axis=-1) @ w_out
    h = h + swiglu(rmsnorm(h, norm_ffn), w_gu, w_down)
    return h, new_kv, new_pool


def odd_layer(h, conv_buf, norm_mix, w_pw1, b_pw1, conv_w, conv_b, cnorm_g, cnorm_b,
              w_pw2, b_pw2, norm_ffn, router_w, w_gu, w_down):
    B, S, _ = h.shape
    a = rmsnorm(h, norm_mix) @ w_pw1 + b_pw1
    u = a[..., :CONV_CH] * jax.nn.sigmoid(a[..., CONV_CH:])
    if conv_buf is None:
        conv_buf = jnp.zeros((B, CONV_W - 1, CONV_CH), u.dtype)
    z = jnp.concatenate([conv_buf.astype(u.dtype), u], axis=1)
    c = lax.conv_general_dilated(z, conv_w[:, None, :].astype(z.dtype), (1,), 'VALID',
                                 dimension_numbers=('NWC', 'WIO', 'NWC'),
                                 feature_group_count=CONV_CH) + conv_b
    n = layernorm(c, cnorm_g, cnorm_b)
    h = h + (jax.nn.silu(n) @ w_pw2 + b_pw2)
    h = h + moe_swiglu(rmsnorm(h, norm_ffn), router_w, w_gu, w_down)
    return h, z[:, -(CONV_W - 1):]


def setup_inputs(seed: int = 0) -> dict:
    key = jax.random.key(seed)
    keys = list(jax.random.split(key, 40))
    f32 = jnp.float32

    def nrm(shape, scale=1.0):
        return jax.random.normal(keys.pop(), shape, f32) * scale

    def gain(shape):
        return 1.0 + 0.02 * jax.random.normal(keys.pop(), shape, f32)

    n_pages = PAST_LEN // PAGE_SIZE
    n_used = DEC_BATCH * n_pages
    n_pool = n_used + n_used // 4
    page_table = jax.random.permutation(keys.pop(), n_pool)[:n_used].reshape(DEC_BATCH, n_pages).astype(jnp.int32)
    E, O = N_EVEN, N_ODD
    x_prompt = nrm((BATCH, SEQ, D_MODEL))
    x_sample = nrm((DEC_BATCH, DEC_SEQ, D_MODEL))
    cache_mla = nrm((E, n_pool, PAGE_SIZE, KV_LORA + QK_ROPE))
    return {
        'x_prompt': x_prompt,
        'x_sample': x_sample,
        'cache_mla': cache_mla,
        'page_table': page_table,
        'state_pool': nrm((E, DEC_BATCH, POOL_BUF, POOL_WIDTH)),
        'state_conv': nrm((O, DEC_BATCH, CONV_W - 1, CONV_CH), 0.5),
        'meta_tokens': nrm((N_META, D_MODEL)),
        'norm_mix_e': gain((E, D_MODEL)),
        'w_in_e': nrm((E, D_MODEL, IN_WIDTH_EVEN), D_MODEL ** -0.5),
        'pool_w': nrm((E, POOL_GROUPS, POOL_GW, POOL_GW), POOL_GW ** -0.5),
        'pool_scale': gain((E, POOL_WIDTH)),
        'q_norm': gain((E, Q_LORA)),
        'kv_norm': gain((E, KV_LORA)),
        'w_uq': nrm((E, Q_LORA, N_HEADS * (QK_NOPE + QK_ROPE)), Q_LORA ** -0.5),
        'w_uk': nrm((E, KV_LORA, N_HEADS, QK_NOPE), KV_LORA ** -0.5),
        'w_uv': nrm((E, KV_LORA, N_HEADS, V_HEAD), KV_LORA ** -0.5),
        'w_out_e': nrm((E, MIX_WIDTH_EVEN, D_MODEL), MIX_WIDTH_EVEN ** -0.5),
        'norm_ffn_e': gain((E, D_MODEL)),
        'w_gu_e': nrm((E, D_MODEL, 2 * D_FF), D_MODEL ** -0.5),
        'w_down_e': nrm((E, D_FF, D_MODEL), D_FF ** -0.5),
        'norm_mix_o': gain((O, D_MODEL)),
        'w_pw1': nrm((O, D_MODEL, 2 * CONV_CH), D_MODEL ** -0.5),
        'b_pw1': nrm((O, 2 * CONV_CH), 0.02),
        'conv_w': nrm((O, CONV_W, CONV_CH), CONV_W ** -0.5),
        'conv_b': nrm((O, CONV_CH), 0.02),
        'cnorm_g': gain((O, CONV_CH)),
        'cnorm_b': nrm((O, CONV_CH), 0.02),
        'w_pw2': nrm((O, CONV_CH, D_MODEL), CONV_CH ** -0.5),
        'b_pw2': nrm((O, D_MODEL), 0.02),
        'norm_ffn_o': gain((O, D_MODEL)),
        'router_w': nrm((O, D_MODEL, N_EXPERTS), D_MODEL ** -0.5),
        'w_gu_x': nrm((O, N_EXPERTS, D_MODEL, 2 * EXPERT_FF), D_MODEL ** -0.5),
        'w_down_x': nrm((O, N_EXPERTS, EXPERT_FF, D_MODEL), EXPERT_FF ** -0.5),
        'final_norm': gain((D_MODEL,)),
    }


def reference(x_prompt, x_sample, cache_mla, page_table, state_pool, state_conv, meta_tokens,
              norm_mix_e, w_in_e, pool_w, pool_scale, q_norm, kv_norm, w_uq, w_uk, w_uv, w_out_e,
              norm_ffn_e, w_gu_e, w_down_e, norm_mix_o, w_pw1, b_pw1, conv_w, conv_b, cnorm_g,
              cnorm_b, w_pw2, b_pw2, norm_ffn_o, router_w, w_gu_x, w_down_x, final_norm):
    Bp = x_prompt.shape[0]
    Bd, Sd = x_sample.shape[:2]
    past_len = page_table.shape[1] * PAGE_SIZE
    meta = jnp.broadcast_to(meta_tokens[None].astype(x_prompt.dtype), (Bp, N_META, D_MODEL))
    hp = jnp.concatenate([meta, x_prompt], axis=1)
    hs = x_sample
    pos_p = jnp.arange(hp.shape[1], dtype=jnp.float32)
    pos_s = jnp.arange(Sd, dtype=jnp.float32) + float(past_len)
    mla_p, mla_s, pool_p, pool_s, conv_p, conv_s = [], [], [], [], [], []
    for layer in range(DEPTH):
        li = layer // 2
        if layer % 2 == 0:
            wts = (norm_mix_e[li], w_in_e[li], pool_w[li], pool_scale[li], q_norm[li], kv_norm[li],
                   w_uq[li], w_uk[li], w_uv[li], w_out_e[li], norm_ffn_e[li], w_gu_e[li], w_down_e[li])
            past = cache_mla[li, page_table].reshape(Bd, past_len, KV_LORA + QK_ROPE)
            hp, kv_new_p, pool_new_p = even_layer(hp, pos_p, None, None, *wts)
            hs, kv_new_s, pool_new_s = even_layer(hs, pos_s, state_pool[li], past, *wts)
            mla_p.append(kv_new_p)
            mla_s.append(kv_new_s)
            pool_p.append(pool_new_p)
            pool_s.append(pool_new_s)
        else:
            wts = (norm_mix_o[li], w_pw1[li], b_pw1[li], conv_w[li], conv_b[li], cnorm_g[li],
                   cnorm_b[li], w_pw2[li], b_pw2[li], norm_ffn_o[li], router_w[li], w_gu_x[li], w_down_x[li])
            hp, conv_new_p = odd_layer(hp, None, *wts)
            hs, conv_new_s = odd_layer(hs, state_conv[li], *wts)
            conv_p.append(conv_new_p)
            conv_s.append(conv_new_s)
    y_prompt = rmsnorm(hp, final_norm)[:, N_META:]
    y_sample = rmsnorm(hs, final_norm)
    new_mla_prompt = jnp.stack(mla_p)
    new_mla_sample = jnp.stack(mla_s)
    new_pool_prompt = jnp.stack(pool_p)
    new_pool_sample = jnp.stack(pool_s)
    new_conv_prompt = jnp.stack(conv_p)
    new_conv_sample = jnp.stack(conv_s)
    return (y_prompt, y_sample, new_mla_prompt, new_mla_sample, new_pool_prompt, new_pool_sample, new_conv_prompt, new_conv_sample)
```

```python
import functools

import jax
import jax.numpy as jnp
from jax import lax
from jax.experimental import pallas as pl
from jax.experimental.pallas import tpu as pltpu

F32 = jnp.float32
BF16 = jnp.bfloat16

D_MODEL = 1024
BATCH = 16
SEQ = 2048
DEC_BATCH = 128
DEC_SEQ = 4
PAGE_SIZE = 128
N_META = 16
POOL_WINDOWS = (2, 4, 8, 16)
POOL_WIDTH = 512
POOL_GW = 128
POOL_BUF = 15
N_HEADS = 8
QK_NOPE = 64
QK_ROPE = 32
V_HEAD = 64
Q_LORA = 384
KV_LORA = 256
ROPE_THETA = 10000.0
ATTN_WIDTH = N_HEADS * V_HEAD
ATTN_SCALE = (QK_NOPE + QK_ROPE) ** -0.5
CONV_CH = D_MODEL
CONV_W = 31
D_FF = 2816
N_EXPERTS = 8
EXPERT_FF = 1408
EPS = 1e-6
NEG_INF = -1e30

LANES = 128
TM = 256
N_R = BATCH * SEQ
N_S = DEC_BATCH * DEC_SEQ
N_RS = N_R + N_S
R_TILES = N_R // TM
S_TILES = N_S // TM
RS_TILES = R_TILES + S_TILES
SM_ROWS = N_S + TM
NF = N_R + SM_ROWS
F_TILES = NF // TM
META_ROW = N_RS
SEQ_TILES = SEQ // TM
IN_W_PAD = 1280
KCAT = KV_LORA + LANES
QCAT = N_HEADS * KCAT
TQ = 256
PAGES_PER_STEP = 16
VMEM_LIMIT = 56 * 1024 * 1024


def _cparams(sem):
    return pltpu.CompilerParams(dimension_semantics=sem, vmem_limit_bytes=VMEM_LIMIT)


def _const_spec(shape):
    nd = len(shape)
    return pl.BlockSpec(shape, lambda *_: (0,) * nd)


def _rms(x, g):
    return x * lax.rsqrt(jnp.mean(x * x, axis=-1, keepdims=True) + EPS) * g


def _silu(x):
    return x * jax.nn.sigmoid(x)


def _dot(a, b):
    return jnp.dot(a, b, preferred_element_type=F32)


def _dot_nt(a, b):
    return lax.dot_general(a, b, (((1,), (1,)), ((), ())), preferred_element_type=F32)


def _r_idx(i):
    return jnp.minimum(i, R_TILES - 1)


def _sm_idx(i):
    return jnp.maximum(i - R_TILES, 0)


def _inproj_kernel(xr_ref, xsm_ref, gmix_ref, win_ref, qn_ref, kvn_ref, wuq_ref, wukt_ref,
                   cos_ref, sin_ref, upool_ref, qcat_ref, kcat_ref, newkv_ref):
    i = pl.program_id(0)
    x = jnp.where(i < R_TILES, xr_ref[...], xsm_ref[...])
    xn = _rms(x, gmix_ref[...]).astype(BF16)
    a = _dot(xn, win_ref[...])
    upool_ref[...] = a[:, :POOL_WIDTH]
    c0 = POOL_WIDTH
    c1 = c0 + Q_LORA
    c2 = c1 + KV_LORA
    cqn = _rms(a[:, c0:c1], qn_ref[...]).astype(BF16)
    q = _dot(cqn, wuq_ref[...])
    ckv = _rms(a[:, c1:c2], kvn_ref[...])
    cosv = cos_ref[...]
    sinv = sin_ref[...]
    lane = lax.broadcasted_iota(jnp.int32, (TM, LANES), 1)
    half = QK_ROPE // 2

    def rope(v):
        rolled = jnp.where(lane < half, pltpu.roll(v, LANES - half, 1), pltpu.roll(v, half, 1))
        return v * cosv + rolled * sinv

    kr = rope(a[:, c2:c2 + LANES])
    newkv_ref[:, :KV_LORA] = ckv
    newkv_ref[:, KV_LORA:] = kr[:, :QK_ROPE]
    kcat_ref[:, :KV_LORA] = ckv.astype(BF16)
    kcat_ref[:, KV_LORA:] = kr.astype(BF16)
    for h in range(N_HEADS):
        qh = q[:, h * LANES:(h + 1) * LANES]
        qlat = _dot(qh.astype(BF16), wukt_ref[h])
        qcat_ref[:, h * KCAT:h * KCAT + KV_LORA] = qlat.astype(BF16)
        qcat_ref[:, h * KCAT + KV_LORA:(h + 1) * KCAT] = rope(qh).astype(BF16)


def _inproj(x_r, x_sm, gmix, w_in_p, q_norm, kv_norm, w_uq_p, w_ukt_p, cos_t, sin_t):
    def tab_idx(i):
        return jnp.where(i < R_TILES, i % SEQ_TILES, i - R_TILES + SEQ_TILES)

    row = lambda w: pl.BlockSpec((TM, w), lambda i: (i, 0))
    return pl.pallas_call(
        _inproj_kernel,
        grid=(F_TILES,),
        in_specs=[
            pl.BlockSpec((TM, D_MODEL), lambda i: (_r_idx(i), 0)),
            pl.BlockSpec((TM, D_MODEL), lambda i: (_sm_idx(i), 0)),
            _const_spec((1, D_MODEL)),
            _const_spec((D_MODEL, IN_W_PAD)),
            _const_spec((1, Q_LORA)),
            _const_spec((1, KV_LORA)),
            _const_spec((Q_LORA, N_HEADS * LANES)),
            _const_spec((N_HEADS, LANES, KV_LORA)),
            pl.BlockSpec((TM, LANES), lambda i: (tab_idx(i), 0)),
            pl.BlockSpec((TM, LANES), lambda i: (tab_idx(i), 0)),
        ],
        out_specs=[row(POOL_WIDTH), row(QCAT), row(KCAT), row(KV_LORA + QK_ROPE)],
        out_shape=[
            jax.ShapeDtypeStruct((NF, POOL_WIDTH), F32),
            jax.ShapeDtypeStruct((NF, QCAT), BF16),
            jax.ShapeDtypeStruct((NF, KCAT), BF16),
            jax.ShapeDtypeStruct((NF, KV_LORA + QK_ROPE), F32),
        ],
        compiler_params=_cparams(("arbitrary",)),
        name="inproj",
    )(x_r, x_sm, gmix, w_in_p, q_norm, kv_norm, w_uq_p, w_ukt_p, cos_t, sin_t)


POOL_RC = 256
POOL_HALO = 16


def _pool_seq_kernel(u_ref, halo_ref, pw_ref, scale_ref, out_ref, z_ref, *, seq_len, pos0):
    z_ref[0:POOL_HALO, :] = halo_ref[...]
    z_ref[POOL_HALO:POOL_HALO + seq_len, :] = u_ref[...]
    rc = min(POOL_RC, seq_len)
    for g, w in enumerate(POOL_WINDOWS):
        sl = slice(g * POOL_GW, (g + 1) * POOL_GW)
        for r0 in range(0, seq_len, rc):
            base = POOL_HALO + r0
            cur = z_ref[base:base + rc, sl]
            acc = cur
            for k in range(1, w):
                acc = acc + z_ref[base - k:base - k + rc, sl]
            if pos0 + 1 >= w:
                mean = acc / float(w)
            else:
                pos = lax.broadcasted_iota(jnp.int32, (rc, POOL_GW), 0) + (pos0 + r0)
                mean = acc / jnp.minimum(pos + 1, w).astype(F32)
            d = (mean - cur).astype(BF16)
            o = _dot(d, pw_ref[g]) * scale_ref[:, sl]
            out_ref[r0:r0 + rc, sl] = o.astype(out_ref.dtype)


def _pool_seq(u_flat, halo, pool_w, pool_scale, *, n_seq, seq_len, first_block, halo_block, pos0):
    kern = functools.partial(_pool_seq_kernel, seq_len=seq_len, pos0=pos0)
    return pl.pallas_call(
        kern,
        grid=(n_seq,),
        in_specs=[
            pl.BlockSpec((seq_len, POOL_WIDTH), lambda b: (first_block + b, 0)),
            pl.BlockSpec((POOL_HALO, POOL_WIDTH), lambda b: (halo_block, 0)),
            _const_spec((len(POOL_WINDOWS), POOL_GW, POOL_GW)),
            _const_spec((1, POOL_WIDTH)),
        ],
        out_specs=pl.BlockSpec((seq_len, POOL_WIDTH), lambda b: (b, 0)),
        out_shape=jax.ShapeDtypeStruct((n_seq * seq_len, POOL_WIDTH), BF16),
        scratch_shapes=[pltpu.VMEM((POOL_HALO + seq_len, POOL_WIDTH), F32)],
        compiler_params=_cparams(("arbitrary",)),
        name="pool_seq",
    )(u_flat, halo, pool_w, pool_scale)


def _pool_step_kernel(st_ref, u_ref, pw_ref, scale_ref, out_ref):
    rows = [st_ref[j] for j in range(POOL_BUF)] + [u_ref[t] for t in range(DEC_SEQ)]
    for t in range(DEC_SEQ):
        cur = rows[POOL_BUF + t]
        for g, w in enumerate(POOL_WINDOWS):
            sl = slice(g * POOL_GW, (g + 1) * POOL_GW)
            acc = cur[:, sl]
            for k in range(1, w):
                acc = acc + rows[POOL_BUF + t - k][:, sl]
            d = (acc / float(w) - cur[:, sl]).astype(BF16)
            o = _dot(d, pw_ref[g]) * scale_ref[:, sl]
            out_ref[t, :, sl] = o.astype(out_ref.dtype)


def _pool_step(state_t, u_t, pool_w, pool_scale):
    return pl.pallas_call(
        _pool_step_kernel,
        out_shape=jax.ShapeDtypeStruct((DEC_SEQ, DEC_BATCH, POOL_WIDTH), BF16),
        compiler_params=pltpu.CompilerParams(vmem_limit_bytes=VMEM_LIMIT),
        name="pool_step",
    )(state_t, u_t, pool_w, pool_scale)


def _softmax_update(carry, s, v):
    m, l, acc = carry
    m_new = jnp.maximum(m, jnp.max(s, axis=1, keepdims=True))
    alpha = jnp.exp(m - m_new)
    p = jnp.exp(s - m_new)
    l = alpha * l + jnp.sum(p, axis=1, keepdims=True)
    acc = alpha * acc + _dot(p.astype(BF16), v)
    return m_new, l, acc


def _softmax_init(s, v):
    m = jnp.max(s, axis=1, keepdims=True)
    p = jnp.exp(s - m)
    return m, jnp.sum(p, axis=1, keepdims=True), _dot(p.astype(BF16), v)


def _attn_prompt_kernel(q_ref, k_ref, km_ref, wuv_ref, o_ref):
    qi = pl.program_id(1)
    km = km_ref[...]
    row = lax.broadcasted_iota(jnp.int32, (TQ, TQ), 0)
    col = lax.broadcasted_iota(jnp.int32, (TQ, TQ), 1)
    causal = col <= row
    outs = []
    for h in range(N_HEADS):
        q = q_ref[:, h * KCAT:(h + 1) * KCAT]
        carry = _softmax_init(_dot_nt(q, km) * ATTN_SCALE, km[:, :KV_LORA])

        def body(j, c, q=q):
            kb = k_ref[pl.ds(pl.multiple_of(j * TQ, TQ), TQ), :]
            return _softmax_update(c, _dot_nt(q, kb) * ATTN_SCALE, kb[:, :KV_LORA])

        carry = lax.fori_loop(0, qi, body, carry)
        kb = k_ref[pl.ds(pl.multiple_of(qi * TQ, TQ), TQ), :]
        s = jnp.where(causal, _dot_nt(q, kb) * ATTN_SCALE, NEG_INF)
        m, l, acc = _softmax_update(carry, s, kb[:, :KV_LORA])
        outs.append(_dot((acc / l).astype(BF16), wuv_ref[h]))
    o_ref[...] = jnp.concatenate(outs, axis=1).astype(o_ref.dtype)


def _attn_prompt(qcat, kcat, w_uv_h):
    nq = SEQ // TQ
    return pl.pallas_call(
        _attn_prompt_kernel,
        grid=(BATCH, nq),
        in_specs=[
            pl.BlockSpec((TQ, QCAT), lambda b, i: (b * nq + i, 0)),
            pl.BlockSpec((SEQ, KCAT), lambda b, i: (b, 0)),
            pl.BlockSpec((N_META, KCAT), lambda b, i: (META_ROW // N_META, 0)),
            _const_spec((N_HEADS, KV_LORA, V_HEAD)),
        ],
        out_specs=pl.BlockSpec((TQ, ATTN_WIDTH), lambda b, i: (b * nq + i, 0)),
        out_shape=jax.ShapeDtypeStruct((N_R, ATTN_WIDTH), BF16),
        compiler_params=_cparams(("arbitrary", "arbitrary")),
        name="attn_prompt",
    )(qcat, kcat, kcat, w_uv_h)


def _attn_meta_kernel(q_ref, km_ref, wuv_ref, o_ref):
    km = km_ref[...]
    row = lax.broadcasted_iota(jnp.int32, (N_META, N_META), 0)
    col = lax.broadcasted_iota(jnp.int32, (N_META, N_META), 1)
    outs = []
    for h in range(N_HEADS):
        q = q_ref[:, h * KCAT:(h + 1) * KCAT]
        s = jnp.where(col <= row, _dot_nt(q, km) * ATTN_SCALE, NEG_INF)
        m, l, acc = _softmax_init(s, km[:, :KV_LORA])
        outs.append(_dot((acc / l).astype(BF16), wuv_ref[h]))
    o_ref[...] = jnp.concatenate(outs, axis=1).astype(o_ref.dtype)


def _attn_meta(qcat, kcat, w_uv_h):
    blk = META_ROW // N_META
    return pl.pallas_call(
        _attn_meta_kernel,
        grid=(1,),
        in_specs=[
            pl.BlockSpec((N_META, QCAT), lambda i: (blk, 0)),
            pl.BlockSpec((N_META, KCAT), lambda i: (blk, 0)),
            _const_spec((N_HEADS, KV_LORA, V_HEAD)),
        ],
        out_specs=pl.BlockSpec((N_META, ATTN_WIDTH), lambda i: (0, 0)),
        out_shape=jax.ShapeDtypeStruct((N_META, ATTN_WIDTH), BF16),
        compiler_params=_cparams(("arbitrary",)),
        name="attn_meta",
    )(qcat, kcat, w_uv_h)


DEC_ROWS = N_HEADS * DEC_SEQ
DEC_NEW_PAD = 16
CHUNK_KEYS = PAGES_PER_STEP * PAGE_SIZE


def _attn_step_kernel(pt_ref, q_ref, knew_ref, *rest):
    page_refs = rest[:PAGES_PER_STEP]
    o_ref = rest[PAGES_PER_STEP]
    kbuf, m_ref, l_ref, acc_ref = rest[PAGES_PER_STEP + 1:]
    c = pl.program_id(1)
    q = q_ref[0]

    @pl.when(c == 0)
    def _():
        kn = knew_ref[0]
        row = lax.broadcasted_iota(jnp.int32, (DEC_ROWS, DEC_NEW_PAD), 0)
        col = lax.broadcasted_iota(jnp.int32, (DEC_ROWS, DEC_NEW_PAD), 1)
        s = jnp.where(col <= row % DEC_SEQ, _dot_nt(q, kn) * ATTN_SCALE, NEG_INF)
        m, l, acc = _softmax_init(s, kn[:, :KV_LORA])
        m_ref[...] = m
        l_ref[...] = l
        acc_ref[...] = acc

    kbuf[:, KV_LORA:] = jnp.zeros((CHUNK_KEYS, KCAT - KV_LORA), BF16)
    for p in range(PAGES_PER_STEP):
        pg = page_refs[p][0, 0]
        kbuf[p * PAGE_SIZE:(p + 1) * PAGE_SIZE, :KV_LORA] = pg[:, :KV_LORA].astype(BF16)
        kbuf[p * PAGE_SIZE:(p + 1) * PAGE_SIZE, KV_LORA:KV_LORA + QK_ROPE] = pg[:, KV_LORA:].astype(BF16)
    kb = kbuf[...]
    m, l, acc = _softmax_update((m_ref[...], l_ref[...], acc_ref[...]),
                                _dot_nt(q, kb) * ATTN_SCALE, kb[:, :KV_LORA])
    m_ref[...] = m
    l_ref[...] = l
    acc_ref[...] = acc

    @pl.when(c == pl.num_programs(1) - 1)
    def _():
        o_ref[0] = (acc / l).astype(o_ref.dtype)


def _attn_step(page_table, q_s, k_new, cache):
    n_pages = page_table.shape[1]
    n_chunks = n_pages // PAGES_PER_STEP
    kv_w = cache.shape[-1]

    def page_spec(p):
        return pl.BlockSpec((1, 1, PAGE_SIZE, kv_w),
                            lambda b, c, pt: (0, pt[b, c * PAGES_PER_STEP + p], 0, 0))

    grid_spec = pltpu.PrefetchScalarGridSpec(
        num_scalar_prefetch=1,
        grid=(DEC_BATCH, n_chunks),
        in_specs=[
            pl.BlockSpec((1, DEC_ROWS, KCAT), lambda b, c, pt: (b, 0, 0)),
            pl.BlockSpec((1, DEC_NEW_PAD, KCAT), lambda b, c, pt: (b, 0, 0)),
        ] + [page_spec(p) for p in range(PAGES_PER_STEP)],
        out_specs=pl.BlockSpec((1, DEC_ROWS, KV_LORA), lambda b, c, pt: (b, 0, 0)),
        scratch_shapes=[
            pltpu.VMEM((CHUNK_KEYS, KCAT), BF16),
            pltpu.VMEM((DEC_ROWS, 1), F32),
            pltpu.VMEM((DEC_ROWS, 1), F32),
            pltpu.VMEM((DEC_ROWS, KV_LORA), F32),
        ],
    )
    return pl.pallas_call(
        _attn_step_kernel,
        grid_spec=grid_spec,
        out_shape=jax.ShapeDtypeStruct((DEC_BATCH, DEC_ROWS, KV_LORA), BF16),
        compiler_params=_cparams(("arbitrary", "arbitrary")),
        name="attn_step",
    )(page_table, q_s, k_new, *([cache] * PAGES_PER_STEP))


def _uv_kernel(o_ref, wuv_ref, out_ref):
    outs = [_dot(o_ref[:, h * KV_LORA:(h + 1) * KV_LORA], wuv_ref[h]) for h in range(N_HEADS)]
    out_ref[...] = jnp.concatenate(outs, axis=1).astype(out_ref.dtype)


def _uv_proj(o_lat, w_uv_h):
    return pl.pallas_call(
        _uv_kernel,
        out_shape=jax.ShapeDtypeStruct((o_lat.shape[0], ATTN_WIDTH), BF16),
        compiler_params=pltpu.CompilerParams(vmem_limit_bytes=VMEM_LIMIT),
        name="uv_proj",
    )(o_lat, w_uv_h)


def _mix_ffn_kernel(xr_ref, xsm_ref, pr_ref, psm_ref, ar_ref, asm_ref, wout_ref, g_ref, wgu_ref,
                    wd_ref, h_ref):
    i = pl.program_id(0)
    is_r = i < R_TILES
    x = jnp.where(is_r, xr_ref[...], xsm_ref[...])
    mix = jnp.concatenate([jnp.where(is_r, pr_ref[...], psm_ref[...]),
                           jnp.where(is_r, ar_ref[...], asm_ref[...])], axis=1)
    h1 = x + _dot(mix, wout_ref[...])
    xn = _rms(h1, g_ref[...]).astype(BF16)
    gu = _dot(xn, wgu_ref[...])
    act = (_silu(gu[:, :D_FF]) * gu[:, D_FF:]).astype(BF16)
    h_ref[...] = h1 + _dot(act, wd_ref[...])


def _mix_ffn(x_r, x_sm, pool_r, pool_sm, attn_r, attn_sm, w_out, g, w_gu, w_down):
    r_spec = lambda w: pl.BlockSpec((TM, w), lambda i: (_r_idx(i), 0))
    sm_spec = lambda w: pl.BlockSpec((TM, w), lambda i: (_sm_idx(i), 0))
    return pl.pallas_call(
        _mix_ffn_kernel,
        grid=(F_TILES,),
        in_specs=[
            r_spec(D_MODEL), sm_spec(D_MODEL),
            r_spec(POOL_WIDTH), sm_spec(POOL_WIDTH),
            r_spec(ATTN_WIDTH), sm_spec(ATTN_WIDTH),
            _const_spec((D_MODEL, D_MODEL)),
            _const_spec((1, D_MODEL)),
            _const_spec((D_MODEL, 2 * D_FF)),
            _const_spec((D_FF, D_MODEL)),
        ],
        out_specs=pl.BlockSpec((TM, D_MODEL), lambda i: (i, 0)),
        out_shape=jax.ShapeDtypeStruct((NF, D_MODEL), F32),
        compiler_params=_cparams(("arbitrary",)),
        name="mix_ffn",
    )(x_r, x_sm, pool_r, pool_sm, attn_r, attn_sm, w_out, g, w_gu, w_down)


def _glu_kernel(h_ref, g_ref, w_ref, b_ref, u_ref):
    xn = _rms(h_ref[...], g_ref[...]).astype(BF16)
    a = _dot(xn, w_ref[...]) + b_ref[...]
    u_ref[...] = a[:, :CONV_CH] * jax.nn.sigmoid(a[:, CONV_CH:])


def _glu(h, g, w_pw1, b_pw1):
    return pl.pallas_call(
        _glu_kernel,
        grid=(F_TILES,),
        in_specs=[
            pl.BlockSpec((TM, D_MODEL), lambda i: (i, 0)),
            _const_spec((1, D_MODEL)),
            _const_spec((D_MODEL, 2 * CONV_CH)),
            _const_spec((1, 2 * CONV_CH)),
        ],
        out_specs=pl.BlockSpec((TM, CONV_CH), lambda i: (i, 0)),
        out_shape=jax.ShapeDtypeStruct((NF, CONV_CH), F32),
        compiler_params=_cparams(("arbitrary",)),
        name="glu",
    )(h, g, w_pw1, b_pw1)


CONV_HALO = 32


def _ln_swish_pw2(c, h, cg_ref, cb_ref, w2_ref, b2_ref):
    xc = c - jnp.mean(c, axis=-1, keepdims=True)
    var = jnp.mean(xc * xc, axis=-1, keepdims=True)
    n = xc * lax.rsqrt(var + EPS) * cg_ref[...] + cb_ref[...]
    return h + (_dot(_silu(n).astype(BF16), w2_ref[...]) + b2_ref[...])


def _conv_seq_kernel(u_ref, prev_ref, halo0_ref, h_ref, cw_ref, cbias_ref, cg_ref, cb_ref, w2_ref,
                     b2_ref, out_ref, z_ref, c_ref):
    i = pl.program_id(0)
    z_ref[0:CONV_HALO, :] = jnp.where(i % SEQ_TILES == 0, halo0_ref[...], prev_ref[...])
    z_ref[CONV_HALO:, :] = u_ref[...]
    off = CONV_HALO - (CONV_W - 1)
    for cc in range(CONV_CH // LANES):
        sl = slice(cc * LANES, (cc + 1) * LANES)
        acc = z_ref[off:off + TM, sl] * cw_ref[0:1, sl]
        for k in range(1, CONV_W):
            acc = acc + z_ref[off + k:off + k + TM, sl] * cw_ref[k:k + 1, sl]
        c_ref[:, sl] = acc + cbias_ref[:, sl]
    out_ref[...] = _ln_swish_pw2(c_ref[...], h_ref[...], cg_ref, cb_ref, w2_ref, b2_ref)


def _conv_seq(u, halo0, h, conv_w, conv_b, cn_g, cn_b, w_pw2, b_pw2):
    per = TM // CONV_HALO
    vec = _const_spec((1, CONV_CH))
    return pl.pallas_call(
        _conv_seq_kernel,
        grid=(R_TILES,),
        in_specs=[
            pl.BlockSpec((TM, CONV_CH), lambda i: (i, 0)),
            pl.BlockSpec((CONV_HALO, CONV_CH), lambda i: (jnp.maximum(i * per - 1, 0), 0)),
            _const_spec((CONV_HALO, CONV_CH)),
            pl.BlockSpec((TM, D_MODEL), lambda i: (i, 0)),
            _const_spec((CONV_W, CONV_CH)),
            vec, vec, vec,
            _const_spec((CONV_CH, D_MODEL)),
            vec,
        ],
        out_specs=pl.BlockSpec((TM, D_MODEL), lambda i: (i, 0)),
        out_shape=jax.ShapeDtypeStruct((N_R, D_MODEL), F32),
        scratch_shapes=[pltpu.VMEM((CONV_HALO + TM, CONV_CH), F32), pltpu.VMEM((TM, CONV_CH), F32)],
        compiler_params=_cparams(("arbitrary",)),
        name="conv_seq",
    )(u, u, halo0, h, conv_w, conv_b, cn_g, cn_b, w_pw2, b_pw2)


def _conv_step_kernel(st_ref, u_ref, h_ref, cw_ref, cbias_ref, cg_ref, cb_ref, w2_ref, b2_ref, out_ref):
    n_st = CONV_W - 1
    for t in range(DEC_SEQ):
        acc = None
        for k in range(CONV_W):
            j = t + k
            zr = st_ref[j] if j < n_st else u_ref[j - n_st]
            term = zr * cw_ref[k:k + 1, :]
            acc = term if acc is None else acc + term
        out_ref[t] = _ln_swish_pw2(acc + cbias_ref[...], h_ref[t], cg_ref, cb_ref, w2_ref, b2_ref)


def _conv_step(state_t, u_t, h_t, conv_w, conv_b, cn_g, cn_b, w_pw2, b_pw2):
    return pl.pallas_call(
        _conv_step_kernel,
        out_shape=jax.ShapeDtypeStruct((DEC_SEQ, DEC_BATCH, D_MODEL), F32),
        compiler_params=pltpu.CompilerParams(vmem_limit_bytes=VMEM_LIMIT),
        name="conv_step",
    )(state_t, u_t, h_t, conv_w, conv_b, cn_g, cn_b, w_pw2, b_pw2)


def _router_kernel(hr_ref, hs_ref, g_ref, wr_ref, xn_ref, gate_ref, y_ref):
    i = pl.program_id(0)
    h = jnp.where(i < R_TILES, hr_ref[...], hs_ref[...])
    y_ref[...] = h
    xn = _rms(h, g_ref[...])
    xn_ref[...] = xn.astype(BF16)
    logits = jnp.dot(xn, wr_ref[...], preferred_element_type=F32, precision=lax.Precision.HIGHEST)
    lane = lax.broadcasted_iota(jnp.int32, (TM, LANES), 1)
    logits = jnp.where(lane < N_EXPERTS, logits, -jnp.inf)
    v1 = jnp.max(logits, axis=1, keepdims=True)
    i1 = jnp.min(jnp.where(logits == v1, lane, LANES), axis=1, keepdims=True)
    rest = jnp.where(lane == i1, -jnp.inf, logits)
    v2 = jnp.max(rest, axis=1, keepdims=True)
    i2 = jnp.min(jnp.where(rest == v2, lane, LANES), axis=1, keepdims=True)
    e2 = jnp.exp(v2 - v1)
    den = 1.0 + e2
    gate_ref[...] = jnp.where(lane == i1, 1.0 / den, 0.0) + jnp.where(lane == i2, e2 / den, 0.0)


def _router(h_r, h_s, g, w_router_p):
    return pl.pallas_call(
        _router_kernel,
        grid=(RS_TILES,),
        in_specs=[
            pl.BlockSpec((TM, D_MODEL), lambda i: (_r_idx(i), 0)),
            pl.BlockSpec((TM, D_MODEL), lambda i: (_sm_idx(i), 0)),
            _const_spec((1, D_MODEL)),
            _const_spec((D_MODEL, LANES)),
        ],
        out_specs=[
            pl.BlockSpec((TM, D_MODEL), lambda i: (i, 0)),
            pl.BlockSpec((TM, LANES), lambda i: (i, 0)),
            pl.BlockSpec((TM, D_MODEL), lambda i: (i, 0)),
        ],
        out_shape=[
            jax.ShapeDtypeStruct((N_RS, D_MODEL), BF16),
            jax.ShapeDtypeStruct((N_RS, LANES), F32),
            jax.ShapeDtypeStruct((N_RS, D_MODEL), F32),
        ],
        compiler_params=_cparams(("arbitrary",)),
        name="router",
    )(h_r, h_s, g, w_router_p)


def _experts_kernel(xn_ref, gate_ref, y_ref, wgu_ref, wd_ref, out_ref):
    e = pl.program_id(0)
    gu = _dot(xn_ref[...], wgu_ref[0])
    act = (_silu(gu[:, :EXPERT_FF]) * gu[:, EXPERT_FF:]).astype(BF16)
    ye = _dot(act, wd_ref[0])
    lane = lax.broadcasted_iota(jnp.int32, (TM, LANES), 1)
    ge = jnp.sum(jnp.where(lane == e, gate_ref[...], 0.0), axis=1, keepdims=True)
    out_ref[...] = y_ref[...] + ge * ye


def _experts(xn, gate, y, w_gu_x, w_down_x):
    return pl.pallas_call(
        _experts_kernel,
        grid=(N_EXPERTS, RS_TILES),
        in_specs=[
            pl.BlockSpec((TM, D_MODEL), lambda e, i: (i, 0)),
            pl.BlockSpec((TM, LANES), lambda e, i: (i, 0)),
            pl.BlockSpec((TM, D_MODEL), lambda e, i: (i, 0)),
            pl.BlockSpec((1, D_MODEL, 2 * EXPERT_FF), lambda e, i: (e, 0, 0)),
            pl.BlockSpec((1, EXPERT_FF, D_MODEL), lambda e, i: (e, 0, 0)),
        ],
        out_specs=pl.BlockSpec((TM, D_MODEL), lambda e, i: (i, 0)),
        out_shape=jax.ShapeDtypeStruct((N_RS, D_MODEL), F32),
        input_output_aliases={2: 0},
        compiler_params=_cparams(("arbitrary", "arbitrary")),
        name="experts",
    )(xn, gate, y, w_gu_x, w_down_x)


def _final_norm_kernel(y_ref, g_ref, o_ref):
    o_ref[...] = _rms(y_ref[...], g_ref[...])


def _final_norm(y, g, first_tile, n_tiles):
    return pl.pallas_call(
        _final_norm_kernel,
        grid=(n_tiles,),
        in_specs=[pl.BlockSpec((TM, D_MODEL), lambda i: (first_tile + i, 0)), _const_spec((1, D_MODEL))],
        out_specs=pl.BlockSpec((TM, D_MODEL), lambda i: (i, 0)),
        out_shape=jax.ShapeDtypeStruct((n_tiles * TM, D_MODEL), F32),
        compiler_params=_cparams(("arbitrary",)),
        name="final_norm",
    )(y, g)


def _rope_tables(past_len):
    pos = jnp.concatenate([
        jnp.arange(SEQ, dtype=F32) + float(N_META),
        jnp.tile(jnp.arange(DEC_SEQ, dtype=F32) + float(past_len), DEC_BATCH),
        jnp.arange(N_META, dtype=F32),
        jnp.zeros((TM - N_META,), F32),
    ])
    inv = ROPE_THETA ** (-jnp.arange(0, QK_ROPE, 2, dtype=F32) / QK_ROPE)
    ang = pos[:, None] * inv[None, :]
    cos, sin = jnp.cos(ang), jnp.sin(ang)
    zeros = jnp.zeros((pos.shape[0], LANES - QK_ROPE), F32)
    return (jnp.concatenate([cos, cos, zeros], axis=1),
            jnp.concatenate([-sin, sin, zeros], axis=1))


def kernel(x_prompt, x_sample, cache_mla, page_table, state_pool, state_conv, meta_tokens, norm_mix_e, w_in_e, pool_w, pool_scale, q_norm, kv_norm, w_uq, w_uk, w_uv, w_out_e, norm_ffn_e, w_gu_e, w_down_e, norm_mix_o, w_pw1, b_pw1, conv_w, conv_b, cnorm_g, cnorm_b, w_pw2, b_pw2, norm_ffn_o, router_w, w_gu_x, w_down_x, final_norm):
    assert x_prompt.shape == (BATCH, SEQ, D_MODEL) and x_sample.shape == (DEC_BATCH, DEC_SEQ, D_MODEL)
    assert norm_mix_e.shape[0] == 1 and norm_mix_o.shape[0] == 1
    past_len = page_table.shape[1] * PAGE_SIZE
    vec = lambda v: v.reshape(1, -1).astype(F32)

    w_in_p = jnp.pad(w_in_e[0], ((0, 0), (0, IN_W_PAD - w_in_e.shape[2]))).astype(BF16)
    uq = w_uq[0].reshape(Q_LORA, N_HEADS, QK_NOPE + QK_ROPE)
    w_uq_p = jnp.concatenate(
        [uq[..., QK_NOPE:], uq[..., :QK_NOPE], jnp.zeros((Q_LORA, N_HEADS, LANES - QK_NOPE - QK_ROPE), F32)],
        axis=-1).reshape(Q_LORA, N_HEADS * LANES).astype(BF16)
    ukt = jnp.transpose(w_uk[0], (1, 2, 0))
    w_ukt_p = jnp.pad(ukt, ((0, 0), (QK_ROPE, LANES - QK_NOPE - QK_ROPE), (0, 0))).astype(BF16)
    w_uv_h = jnp.transpose(w_uv[0], (1, 0, 2)).astype(BF16)
    pool_w_b = pool_w[0].astype(BF16)
    w_router_p = jnp.pad(router_w[0], ((0, 0), (0, LANES - N_EXPERTS))).astype(F32)
    cos_t, sin_t = _rope_tables(past_len)

    x_r = x_prompt.reshape(N_R, D_MODEL)
    x_sm = jnp.concatenate([x_sample.reshape(N_S, D_MODEL), meta_tokens.astype(F32),
                            jnp.zeros((TM - N_META, D_MODEL), F32)], axis=0)

    upool, qcat, kcat, newkv = _inproj(x_r, x_sm, vec(norm_mix_e[0]), w_in_p, vec(q_norm[0]),
                                       vec(kv_norm[0]), w_uq_p, w_ukt_p, cos_t, sin_t)
    pscale = vec(pool_scale[0])
    pool_r = _pool_seq(upool, upool, pool_w_b, pscale, n_seq=BATCH, seq_len=SEQ, first_block=0,
                       halo_block=META_ROW // POOL_HALO, pos0=N_META)
    pool_m = _pool_seq(upool, jnp.zeros((POOL_HALO, POOL_WIDTH), F32), pool_w_b, pscale, n_seq=1,
                       seq_len=N_META, first_block=META_ROW // N_META, halo_block=0, pos0=0)
    upool_s = upool[N_R:N_RS].reshape(DEC_BATCH, DEC_SEQ, POOL_WIDTH)
    pool_s = _pool_step(jnp.transpose(state_pool[0], (1, 0, 2)), jnp.transpose(upool_s, (1, 0, 2)),
                        pool_w_b, pscale)
    pool_s = jnp.transpose(pool_s, (1, 0, 2)).reshape(N_S, POOL_WIDTH)
    pool_sm = jnp.concatenate([pool_s, pool_m, jnp.zeros((TM - N_META, POOL_WIDTH), BF16)], axis=0)

    attn_r = _attn_prompt(qcat, kcat, w_uv_h)
    attn_m = _attn_meta(qcat, kcat, w_uv_h)
    q_s = qcat[N_R:N_RS].reshape(DEC_BATCH, DEC_SEQ, N_HEADS, KCAT)
    q_s = jnp.transpose(q_s, (0, 2, 1, 3)).reshape(DEC_BATCH, DEC_ROWS, KCAT)
    k_new = jnp.pad(kcat[N_R:N_RS].reshape(DEC_BATCH, DEC_SEQ, KCAT),
                    ((0, 0), (0, DEC_NEW_PAD - DEC_SEQ), (0, 0)))
    o_lat = _attn_step(page_table.astype(jnp.int32), q_s, k_new, cache_mla)
    o_lat = jnp.transpose(o_lat.reshape(DEC_BATCH, N_HEADS, DEC_SEQ, KV_LORA), (0, 2, 1, 3))
    attn_s = _uv_proj(o_lat.reshape(N_S, N_HEADS * KV_LORA), w_uv_h)
    attn_sm = jnp.concatenate([attn_s, attn_m, jnp.zeros((TM - N_META, ATTN_WIDTH), BF16)], axis=0)

    h2 = _mix_ffn(x_r, x_sm, pool_r, pool_sm, attn_r, attn_sm, w_out_e[0].astype(BF16),
                  vec(norm_ffn_e[0]), w_gu_e[0].astype(BF16), w_down_e[0].astype(BF16))

    u = _glu(h2, vec(norm_mix_o[0]), w_pw1[0].astype(BF16), vec(b_pw1[0]))
    conv_args = (conv_w[0].astype(F32), vec(conv_b[0]), vec(cnorm_g[0]), vec(cnorm_b[0]),
                 w_pw2[0].astype(BF16), vec(b_pw2[0]))
    halo0 = jnp.concatenate([jnp.zeros((CONV_HALO - N_META, CONV_CH), F32),
                             u[META_ROW:META_ROW + N_META]], axis=0)
    h3_r = _conv_seq(u, halo0, h2, *conv_args)
    u_s = u[N_R:N_RS].reshape(DEC_BATCH, DEC_SEQ, CONV_CH)
    h2_s = h2[N_R:N_RS].reshape(DEC_BATCH, DEC_SEQ, D_MODEL)
    h3_s = _conv_step(jnp.transpose(state_conv[0], (1, 0, 2)), jnp.transpose(u_s, (1, 0, 2)),
                      jnp.transpose(h2_s, (1, 0, 2)), *conv_args)
    h3_s = jnp.transpose(h3_s, (1, 0, 2)).reshape(N_S, D_MODEL)

    xn, gate, y0 = _router(h3_r, h3_s, vec(norm_ffn_o[0]), w_router_p)
    y = _experts(xn, gate, y0, w_gu_x[0].astype(BF16), w_down_x[0].astype(BF16))
    fg = vec(final_norm)
    y_prompt = _final_norm(y, fg, 0, R_TILES).reshape(BATCH, SEQ, D_MODEL)
    y_sample = _final_norm(y, fg, R_TILES, S_TILES).reshape(DEC_BATCH, DEC_SEQ, D_MODEL)

    kv_w = KV_LORA + QK_ROPE
    kv_meta = jnp.broadcast_to(newkv[META_ROW:META_ROW + N_META][None], (BATCH, N_META, kv_w))
    new_mla_prompt = jnp.concatenate([kv_meta, newkv[:N_R].reshape(BATCH, SEQ, kv_w)], axis=1)[None]
    new_mla_sample = newkv[N_R:N_RS].reshape(1, DEC_BATCH, DEC_SEQ, kv_w)
    new_pool_prompt = upool[:N_R].reshape(BATCH, SEQ, POOL_WIDTH)[:, SEQ - POOL_BUF:][None]
    new_pool_sample = jnp.concatenate([state_pool[0][:, DEC_SEQ:], upool_s], axis=1)[None]
    new_conv_prompt = u[:N_R].reshape(BATCH, SEQ, CONV_CH)[:, SEQ - (CONV_W - 1):][None]
    new_conv_sample = jnp.concatenate([state_conv[0][:, DEC_SEQ:].astype(F32), u_s], axis=1)[None]
    return (y_prompt, y_sample, new_mla_prompt, new_mla_sample, new_pool_prompt, new_pool_sample,
            new_conv_prompt, new_conv_sample)
```

```python
import functools

import jax
import jax.numpy as jnp
from jax import lax
from jax.experimental import pallas as pl
from jax.experimental.pallas import tpu as pltpu

F32 = jnp.float32
BF16 = jnp.bfloat16

D_MODEL = 1024
BATCH = 16
SEQ = 2048
DEC_BATCH = 128
DEC_SEQ = 4
PAGE_SIZE = 128
N_META = 16
POOL_WINDOWS = (2, 4, 8, 16)
POOL_WIDTH = 512
POOL_GW = 128
POOL_BUF = 15
N_HEADS = 8
QK_NOPE = 64
QK_ROPE = 32
V_HEAD = 64
Q_LORA = 384
KV_LORA = 256
ROPE_THETA = 10000.0
ATTN_WIDTH = N_HEADS * V_HEAD
ATTN_SCALE = (QK_NOPE + QK_ROPE) ** -0.5
CONV_CH = D_MODEL
CONV_W = 31
D_FF = 2816
N_EXPERTS = 8
EXPERT_FF = 1408
EPS = 1e-6
NEG_INF = -1e30

LANES = 128
TM = 256
N_R = BATCH * SEQ
N_S = DEC_BATCH * DEC_SEQ
N_RS = N_R + N_S
R_TILES = N_R // TM
S_TILES = N_S // TM
RS_TILES = R_TILES + S_TILES
SM_ROWS = N_S + TM
NF = N_R + SM_ROWS
F_TILES = NF // TM
META_ROW = N_RS
SEQ_TILES = SEQ // TM
IN_W_PAD = 1280
KCAT = KV_LORA + LANES
QCAT = N_HEADS * KCAT
TQ = 256
PAGES_PER_STEP = 32
VMEM_LIMIT = 56 * 1024 * 1024


def _cparams(sem):
    return pltpu.CompilerParams(dimension_semantics=sem, vmem_limit_bytes=VMEM_LIMIT)


def _const_spec(shape):
    nd = len(shape)
    return pl.BlockSpec(shape, lambda *_: (0,) * nd)


def _rms(x, g):
    return x * lax.rsqrt(jnp.mean(x * x, axis=-1, keepdims=True) + EPS) * g


def _silu(x):
    return x * jax.nn.sigmoid(x)


def _dot(a, b):
    return jnp.dot(a, b, preferred_element_type=F32)


def _dot_nt(a, b):
    return lax.dot_general(a, b, (((1,), (1,)), ((), ())), preferred_element_type=F32)


def _r_idx(i):
    return jnp.minimum(i, R_TILES - 1)


def _sm_idx(i):
    return jnp.maximum(i - R_TILES, 0)


def _inproj_kernel(xr_ref, xsm_ref, gmix_ref, win_ref, qn_ref, kvn_ref, wuq_ref, wuk_ref, wuv_ref,
                   wukt_ref, cos_ref, sin_ref, upool_ref, q_ref, k_ref, v_ref, newkv_ref, qcat_ref):
    i = pl.program_id(0)
    x = jnp.where(i < R_TILES, xr_ref[...], xsm_ref[...])
    xn = _rms(x, gmix_ref[...]).astype(BF16)
    a = _dot(xn, win_ref[...])
    upool_ref[...] = a[:, :POOL_WIDTH]
    c0 = POOL_WIDTH
    c1 = c0 + Q_LORA
    c2 = c1 + KV_LORA
    cqn = _rms(a[:, c0:c1], qn_ref[...]).astype(BF16)
    q = _dot(cqn, wuq_ref[...])
    ckv = _rms(a[:, c1:c2], kvn_ref[...])
    cosv = cos_ref[...]
    sinv = sin_ref[...]
    lane = lax.broadcasted_iota(jnp.int32, (TM, LANES), 1)
    half = QK_ROPE // 2

    def rope(v):
        rolled = jnp.where(lane < half, pltpu.roll(v, LANES - half, 1), pltpu.roll(v, half, 1))
        return v * cosv + rolled * sinv

    kr = rope(a[:, c2:c2 + LANES])
    newkv_ref[:, :KV_LORA] = ckv
    newkv_ref[:, KV_LORA:] = kr[:, :QK_ROPE]
    ckv_b = ckv.astype(BF16)
    k_all = _dot(ckv_b, wuk_ref[...]) + jnp.concatenate([kr] * N_HEADS, axis=1)
    k_ref[...] = k_all.astype(BF16)
    v_ref[...] = _dot(ckv_b, wuv_ref[...]).astype(BF16)
    q_rot = [rope(q[:, h * LANES:(h + 1) * LANES]) for h in range(N_HEADS)]
    for h in range(N_HEADS):
        q_ref[:, h * LANES:(h + 1) * LANES] = q_rot[h].astype(BF16)

    @pl.when(i >= R_TILES)
    def _():
        for h in range(N_HEADS):
            qcat_ref[h, :, :KV_LORA] = _dot(q_rot[h].astype(BF16), wukt_ref[h]).astype(BF16)
            qcat_ref[h, :, KV_LORA:] = jnp.where(lane < QK_ROPE, q_rot[h], 0.0).astype(BF16)


def _inproj(x_r, x_sm, gmix, w_in_p, q_norm, kv_norm, w_uq_p, w_uk_p, w_uv_p, w_ukt_p, cos_t, sin_t):
    def tab_idx(i):
        return jnp.where(i < R_TILES, i % SEQ_TILES, i - R_TILES + SEQ_TILES)

    hw = N_HEADS * LANES
    row = lambda w: pl.BlockSpec((TM, w), lambda i: (i, 0))
    return pl.pallas_call(
        _inproj_kernel,
        grid=(F_TILES,),
        in_specs=[
            pl.BlockSpec((TM, D_MODEL), lambda i: (_r_idx(i), 0)),
            pl.BlockSpec((TM, D_MODEL), lambda i: (_sm_idx(i), 0)),
            _const_spec((1, D_MODEL)),
            _const_spec((D_MODEL, IN_W_PAD)),
            _const_spec((1, Q_LORA)),
            _const_spec((1, KV_LORA)),
            _const_spec((Q_LORA, hw)),
            _const_spec((KV_LORA, hw)),
            _const_spec((KV_LORA, hw)),
            _const_spec((N_HEADS, LANES, KV_LORA)),
            pl.BlockSpec((TM, LANES), lambda i: (tab_idx(i), 0)),
            pl.BlockSpec((TM, LANES), lambda i: (tab_idx(i), 0)),
        ],
        out_specs=[row(POOL_WIDTH), row(hw), row(hw), row(hw), row(KV_LORA + QK_ROPE),
                   pl.BlockSpec((N_HEADS, TM, KCAT), lambda i: (0, _sm_idx(i), 0))],
        out_shape=[
            jax.ShapeDtypeStruct((NF, POOL_WIDTH), F32),
            jax.ShapeDtypeStruct((NF, hw), BF16),
            jax.ShapeDtypeStruct((NF, hw), BF16),
            jax.ShapeDtypeStruct((NF, hw), BF16),
            jax.ShapeDtypeStruct((NF, KV_LORA + QK_ROPE), F32),
            jax.ShapeDtypeStruct((N_HEADS, SM_ROWS, KCAT), BF16),
        ],
        compiler_params=_cparams(("arbitrary",)),
        name="inproj",
    )(x_r, x_sm, gmix, w_in_p, q_norm, kv_norm, w_uq_p, w_uk_p, w_uv_p, w_ukt_p, cos_t, sin_t)


POOL_RC = 256
POOL_HALO = 16


def _pool_seq_kernel(u_ref, halo_ref, pw_ref, scale_ref, out_ref, z_ref, *, seq_len, pos0):
    z_ref[0:POOL_HALO, :] = halo_ref[...]
    z_ref[POOL_HALO:POOL_HALO + seq_len, :] = u_ref[...]
    rc = min(POOL_RC, seq_len)
    for g, w in enumerate(POOL_WINDOWS):
        sl = slice(g * POOL_GW, (g + 1) * POOL_GW)
        for r0 in range(0, seq_len, rc):
            base = POOL_HALO + r0
            cur = z_ref[base:base + rc, sl]
            acc = cur
            for k in range(1, w):
                acc = acc + z_ref[base - k:base - k + rc, sl]
            if pos0 + 1 >= w:
                mean = acc / float(w)
            else:
                pos = lax.broadcasted_iota(jnp.int32, (rc, POOL_GW), 0) + (pos0 + r0)
                mean = acc / jnp.minimum(pos + 1, w).astype(F32)
            d = (mean - cur).astype(BF16)
            o = _dot(d, pw_ref[g]) * scale_ref[:, sl]
            out_ref[r0:r0 + rc, sl] = o.astype(out_ref.dtype)


def _pool_seq(u_flat, halo, pool_w, pool_scale, *, n_seq, seq_len, first_block, halo_block, pos0):
    kern = functools.partial(_pool_seq_kernel, seq_len=seq_len, pos0=pos0)
    return pl.pallas_call(
        kern,
        grid=(n_seq,),
        in_specs=[
            pl.BlockSpec((seq_len, POOL_WIDTH), lambda b: (first_block + b, 0)),
            pl.BlockSpec((POOL_HALO, POOL_WIDTH), lambda b: (halo_block, 0)),
            _const_spec((len(POOL_WINDOWS), POOL_GW, POOL_GW)),
            _const_spec((1, POOL_WIDTH)),
        ],
        out_specs=pl.BlockSpec((seq_len, POOL_WIDTH), lambda b: (b, 0)),
        out_shape=jax.ShapeDtypeStruct((n_seq * seq_len, POOL_WIDTH), BF16),
        scratch_shapes=[pltpu.VMEM((POOL_HALO + seq_len, POOL_WIDTH), F32)],
        compiler_params=_cparams(("arbitrary",)),
        name="pool_seq",
    )(u_flat, halo, pool_w, pool_scale)


def _pool_step_kernel(st_ref, u_ref, pw_ref, scale_ref, out_ref):
    rows = [st_ref[j] for j in range(POOL_BUF)] + [u_ref[t] for t in range(DEC_SEQ)]
    for t in range(DEC_SEQ):
        cur = rows[POOL_BUF + t]
        for g, w in enumerate(POOL_WINDOWS):
            sl = slice(g * POOL_GW, (g + 1) * POOL_GW)
            acc = cur[:, sl]
            for k in range(1, w):
                acc = acc + rows[POOL_BUF + t - k][:, sl]
            d = (acc / float(w) - cur[:, sl]).astype(BF16)
            o = _dot(d, pw_ref[g]) * scale_ref[:, sl]
            out_ref[t, :, sl] = o.astype(out_ref.dtype)


def _pool_step(state_t, u_t, pool_w, pool_scale):
    return pl.pallas_call(
        _pool_step_kernel,
        out_shape=jax.ShapeDtypeStruct((DEC_SEQ, DEC_BATCH, POOL_WIDTH), BF16),
        compiler_params=pltpu.CompilerParams(vmem_limit_bytes=VMEM_LIMIT),
        name="pool_step",
    )(state_t, u_t, pool_w, pool_scale)


def _softmax_update(carry, s, v):
    m, l, acc = carry
    m_new = jnp.maximum(m, jnp.max(s, axis=1, keepdims=True))
    alpha = jnp.exp(m - m_new)
    p = jnp.exp(s - m_new)
    l = alpha * l + jnp.sum(p, axis=1, keepdims=True)
    acc = alpha * acc + _dot(p.astype(BF16), v)
    return m_new, l, acc


def _softmax_init(s, v):
    m = jnp.max(s, axis=1, keepdims=True)
    p = jnp.exp(s - m)
    return m, jnp.sum(p, axis=1, keepdims=True), _dot(p.astype(BF16), v)


def _attn_prompt_kernel(q_ref, k_ref, v_ref, km_ref, vm_ref, o_ref, mpart, mrep, lpart, acc):
    qi = pl.program_id(1)
    lane = lax.broadcasted_iota(jnp.int32, (TQ, LANES), 1)
    meta_ok = lane < N_META
    causal = (lax.broadcasted_iota(jnp.int32, (TQ, TQ), 1)
              <= lax.broadcasted_iota(jnp.int32, (TQ, TQ), 0))
    hs = [slice(h * LANES, (h + 1) * LANES) for h in range(N_HEADS)]

    def keys(j):
        return pl.ds(pl.multiple_of(j * TQ, TQ), TQ)

    def raw_scores(h, kb, mask):
        s = _dot_nt(q_ref[:, hs[h]], kb)
        return s if mask is None else jnp.where(mask, s, NEG_INF)

    def max_sweep(j, mask):
        for h in range(N_HEADS):
            s = raw_scores(h, k_ref[keys(j), hs[h]], mask)
            mpart[h] = jnp.maximum(mpart[h], jnp.maximum(s[:, :LANES], s[:, LANES:]))

    for h in range(N_HEADS):
        mpart[h] = raw_scores(h, km_ref[:, hs[h]], meta_ok)
    pl.loop(0, qi)(lambda j: max_sweep(j, None))
    max_sweep(qi, causal)
    for h in range(N_HEADS):
        m = jnp.max(mpart[h], axis=1, keepdims=True) * ATTN_SCALE
        mrep[h] = jnp.broadcast_to(m, (TQ, LANES))

    def probs(h, kb, mask):
        s = raw_scores(h, kb, mask)
        m = mrep[h]
        if s.shape[1] > LANES:
            m = jnp.concatenate([m] * (s.shape[1] // LANES), axis=1)
        return jnp.exp(s * ATTN_SCALE - m)

    def sum_sweep(j, mask):
        for h in range(N_HEADS):
            p = probs(h, k_ref[keys(j), hs[h]], mask)
            lpart[h] += p[:, :LANES] + p[:, LANES:]
            acc[h] += _dot(p.astype(BF16), v_ref[keys(j), hs[h]])

    for h in range(N_HEADS):
        p = probs(h, km_ref[:, hs[h]], meta_ok)
        lpart[h] = p
        acc[h] = _dot(p.astype(BF16), vm_ref[:, hs[h]])
    pl.loop(0, qi)(lambda j: sum_sweep(j, None))
    sum_sweep(qi, causal)
    outs = [(acc[h] / jnp.sum(lpart[h], axis=1, keepdims=True))[:, :V_HEAD] for h in range(N_HEADS)]
    o_ref[...] = jnp.concatenate(outs, axis=1).astype(o_ref.dtype)


def _attn_prompt(q_all, k_all, v_all, k_meta, v_meta):
    nq = SEQ // TQ
    hw = N_HEADS * LANES
    stat = pltpu.VMEM((N_HEADS, TQ, LANES), F32)
    return pl.pallas_call(
        _attn_prompt_kernel,
        grid=(BATCH, nq),
        in_specs=[
            pl.BlockSpec((TQ, hw), lambda b, i: (b * nq + i, 0)),
            pl.BlockSpec((SEQ, hw), lambda b, i: (b, 0)),
            pl.BlockSpec((SEQ, hw), lambda b, i: (b, 0)),
            _const_spec((LANES, hw)),
            _const_spec((LANES, hw)),
        ],
        out_specs=pl.BlockSpec((TQ, ATTN_WIDTH), lambda b, i: (b * nq + i, 0)),
        out_shape=jax.ShapeDtypeStruct((N_R, ATTN_WIDTH), BF16),
        scratch_shapes=[stat, stat, stat, stat],
        compiler_params=_cparams(("arbitrary", "arbitrary")),
        name="attn_prompt",
    )(q_all, k_all, v_all, k_meta, v_meta)


def _attn_meta_kernel(q_ref, km_ref, vm_ref, o_ref):
    causal = (lax.broadcasted_iota(jnp.int32, (N_META, LANES), 1)
              <= lax.broadcasted_iota(jnp.int32, (N_META, LANES), 0))
    outs = []
    for h in range(N_HEADS):
        sl = slice(h * LANES, (h + 1) * LANES)
        s = jnp.where(causal, _dot_nt(q_ref[:, sl], km_ref[:, sl]), NEG_INF) * ATTN_SCALE
        p = jnp.exp(s - jnp.max(s, axis=1, keepdims=True))
        o = _dot(p.astype(BF16), vm_ref[:, sl]) / jnp.sum(p, axis=1, keepdims=True)
        outs.append(o[:, :V_HEAD])
    o_ref[...] = jnp.concatenate(outs, axis=1).astype(o_ref.dtype)


def _attn_meta(q_all, k_meta, v_meta):
    hw = N_HEADS * LANES
    return pl.pallas_call(
        _attn_meta_kernel,
        grid=(1,),
        in_specs=[
            pl.BlockSpec((N_META, hw), lambda i: (META_ROW // N_META, 0)),
            _const_spec((LANES, hw)),
            _const_spec((LANES, hw)),
        ],
        out_specs=pl.BlockSpec((N_META, ATTN_WIDTH), lambda i: (0, 0)),
        out_shape=jax.ShapeDtypeStruct((N_META, ATTN_WIDTH), BF16),
        compiler_params=_cparams(("arbitrary",)),
        name="attn_meta",
    )(q_all, k_meta, v_meta)


DEC_ROWS = N_HEADS * DEC_SEQ
DEC_NEW_PAD = 16
CHUNK_KEYS = PAGES_PER_STEP * PAGE_SIZE


def _attn_step_kernel(pt_ref, q_ref, knew_ref, *rest):
    page_refs = rest[:PAGES_PER_STEP]
    o_ref = rest[PAGES_PER_STEP]
    kbuf, m_ref, l_ref, acc_ref = rest[PAGES_PER_STEP + 1:]
    c = pl.program_id(1)
    q = q_ref[0]

    @pl.when(c == 0)
    def _():
        kn = knew_ref[0]
        row = lax.broadcasted_iota(jnp.int32, (DEC_ROWS, DEC_NEW_PAD), 0)
        col = lax.broadcasted_iota(jnp.int32, (DEC_ROWS, DEC_NEW_PAD), 1)
        s = jnp.where(col <= row % DEC_SEQ, _dot_nt(q, kn) * ATTN_SCALE, NEG_INF)
        m, l, acc = _softmax_init(s, kn[:, :KV_LORA])
        m_ref[...] = m
        l_ref[...] = l
        acc_ref[...] = acc

    kv_w = KV_LORA + QK_ROPE
    kbuf[kv_w:, :] = jnp.zeros((KCAT - kv_w, CHUNK_KEYS), BF16)
    for p in range(PAGES_PER_STEP):
        kbuf[:kv_w, p * PAGE_SIZE:(p + 1) * PAGE_SIZE] = page_refs[p][0, 0].astype(BF16)
    s = _dot(q, kbuf[...]) * ATTN_SCALE
    m_old = m_ref[...]
    m = jnp.maximum(m_old, jnp.max(s, axis=1, keepdims=True))
    alpha = jnp.exp(m_old - m)
    p_un = jnp.exp(s - m)
    l = alpha * l_ref[...] + jnp.sum(p_un, axis=1, keepdims=True)
    acc = alpha * acc_ref[...] + _dot_nt(p_un.astype(BF16), kbuf[:KV_LORA, :])
    m_ref[...] = m
    l_ref[...] = l
    acc_ref[...] = acc

    @pl.when(c == pl.num_programs(1) - 1)
    def _():
        o_ref[0] = (acc / l).astype(o_ref.dtype)


def _attn_step(page_table, q_s, k_new, cache_t):
    n_pages = page_table.shape[1]
    assert n_pages % PAGES_PER_STEP == 0
    n_chunks = n_pages // PAGES_PER_STEP
    kv_w = cache_t.shape[2]

    def page_spec(p):
        return pl.BlockSpec((1, 1, kv_w, PAGE_SIZE),
                            lambda b, c, pt: (0, pt[b, c * PAGES_PER_STEP + p], 0, 0))

    grid_spec = pltpu.PrefetchScalarGridSpec(
        num_scalar_prefetch=1,
        grid=(DEC_BATCH, n_chunks),
        in_specs=[
            pl.BlockSpec((1, DEC_ROWS, KCAT), lambda b, c, pt: (b, 0, 0)),
            pl.BlockSpec((1, DEC_NEW_PAD, KCAT), lambda b, c, pt: (b, 0, 0)),
        ] + [page_spec(p) for p in range(PAGES_PER_STEP)],
        out_specs=pl.BlockSpec((1, DEC_ROWS, KV_LORA), lambda b, c, pt: (b, 0, 0)),
        scratch_shapes=[
            pltpu.VMEM((KCAT, CHUNK_KEYS), BF16),
            pltpu.VMEM((DEC_ROWS, 1), F32),
            pltpu.VMEM((DEC_ROWS, 1), F32),
            pltpu.VMEM((DEC_ROWS, KV_LORA), F32),
        ],
    )
    return pl.pallas_call(
        _attn_step_kernel,
        grid_spec=grid_spec,
        out_shape=jax.ShapeDtypeStruct((DEC_BATCH, DEC_ROWS, KV_LORA), BF16),
        compiler_params=_cparams(("arbitrary", "arbitrary")),
        name="attn_step",
    )(page_table, q_s, k_new, *([cache_t] * PAGES_PER_STEP))


def _uv_kernel(o_ref, wuv_ref, out_ref):
    outs = [_dot(o_ref[:, h * KV_LORA:(h + 1) * KV_LORA], wuv_ref[h]) for h in range(N_HEADS)]
    out_ref[...] = jnp.concatenate(outs, axis=1).astype(out_ref.dtype)


def _uv_proj(o_lat, w_uv_h):
    return pl.pallas_call(
        _uv_kernel,
        out_shape=jax.ShapeDtypeStruct((o_lat.shape[0], ATTN_WIDTH), BF16),
        compiler_params=pltpu.CompilerParams(vmem_limit_bytes=VMEM_LIMIT),
        name="uv_proj",
    )(o_lat, w_uv_h)


def _mix_ffn_kernel(xr_ref, xsm_ref, pr_ref, psm_ref, ar_ref, asm_ref, wout_ref, g_ref, wgu_ref,
                    wd_ref, h_ref):
    i = pl.program_id(0)
    is_r = i < R_TILES
    x = jnp.where(is_r, xr_ref[...], xsm_ref[...])
    mix = jnp.concatenate([jnp.where(is_r, pr_ref[...], psm_ref[...]),
                           jnp.where(is_r, ar_ref[...], asm_ref[...])], axis=1)
    h1 = x + _dot(mix, wout_ref[...])
    xn = _rms(h1, g_ref[...]).astype(BF16)
    gu = _dot(xn, wgu_ref[...])
    act = (_silu(gu[:, :D_FF]) * gu[:, D_FF:]).astype(BF16)
    h_ref[...] = h1 + _dot(act, wd_ref[...])


def _mix_ffn(x_r, x_sm, pool_r, pool_sm, attn_r, attn_sm, w_out, g, w_gu, w_down):
    r_spec = lambda w: pl.BlockSpec((TM, w), lambda i: (_r_idx(i), 0))
    sm_spec = lambda w: pl.BlockSpec((TM, w), lambda i: (_sm_idx(i), 0))
    return pl.pallas_call(
        _mix_ffn_kernel,
        grid=(F_TILES,),
        in_specs=[
            r_spec(D_MODEL), sm_spec(D_MODEL),
            r_spec(POOL_WIDTH), sm_spec(POOL_WIDTH),
            r_spec(ATTN_WIDTH), sm_spec(ATTN_WIDTH),
            _const_spec((D_MODEL, D_MODEL)),
            _const_spec((1, D_MODEL)),
            _const_spec((D_MODEL, 2 * D_FF)),
            _const_spec((D_FF, D_MODEL)),
        ],
        out_specs=pl.BlockSpec((TM, D_MODEL), lambda i: (i, 0)),
        out_shape=jax.ShapeDtypeStruct((NF, D_MODEL), F32),
        compiler_params=_cparams(("arbitrary",)),
        name="mix_ffn",
    )(x_r, x_sm, pool_r, pool_sm, attn_r, attn_sm, w_out, g, w_gu, w_down)


def _glu_kernel(h_ref, g_ref, w_ref, b_ref, u_ref):
    xn = _rms(h_ref[...], g_ref[...]).astype(BF16)
    a = _dot(xn, w_ref[...]) + b_ref[...]
    u_ref[...] = a[:, :CONV_CH] * jax.nn.sigmoid(a[:, CONV_CH:])


def _glu(h, g, w_pw1, b_pw1):
    return pl.pallas_call(
        _glu_kernel,
        grid=(F_TILES,),
        in_specs=[
            pl.BlockSpec((TM, D_MODEL), lambda i: (i, 0)),
            _const_spec((1, D_MODEL)),
            _const_spec((D_MODEL, 2 * CONV_CH)),
            _const_spec((1, 2 * CONV_CH)),
        ],
        out_specs=pl.BlockSpec((TM, CONV_CH), lambda i: (i, 0)),
        out_shape=jax.ShapeDtypeStruct((NF, CONV_CH), F32),
        compiler_params=_cparams(("arbitrary",)),
        name="glu",
    )(h, g, w_pw1, b_pw1)


CONV_HALO = 32


def _ln_swish_pw2(c, h, cg_ref, cb_ref, w2_ref, b2_ref):
    xc = c - jnp.mean(c, axis=-1, keepdims=True)
    var = jnp.mean(xc * xc, axis=-1, keepdims=True)
    n = xc * lax.rsqrt(var + EPS) * cg_ref[...] + cb_ref[...]
    return h + (_dot(_silu(n).astype(BF16), w2_ref[...]) + b2_ref[...])


def _conv_seq_kernel(u_ref, prev_ref, halo0_ref, h_ref, cw_ref, cbias_ref, cg_ref, cb_ref, w2_ref,
                     b2_ref, out_ref, z_ref, c_ref):
    i = pl.program_id(0)
    z_ref[0:CONV_HALO, :] = jnp.where(i % SEQ_TILES == 0, halo0_ref[...], prev_ref[...])
    z_ref[CONV_HALO:, :] = u_ref[...]
    off = CONV_HALO - (CONV_W - 1)
    for cc in range(CONV_CH // LANES):
        sl = slice(cc * LANES, (cc + 1) * LANES)
        acc = z_ref[off:off + TM, sl] * cw_ref[0:1, sl]
        for k in range(1, CONV_W):
            acc = acc + z_ref[off + k:off + k + TM, sl] * cw_ref[k:k + 1, sl]
        c_ref[:, sl] = acc + cbias_ref[:, sl]
    out_ref[...] = _ln_swish_pw2(c_ref[...], h_ref[...], cg_ref, cb_ref, w2_ref, b2_ref)


def _conv_seq(u, halo0, h, conv_w, conv_b, cn_g, cn_b, w_pw2, b_pw2):
    per = TM // CONV_HALO
    vec = _const_spec((1, CONV_CH))
    return pl.pallas_call(
        _conv_seq_kernel,
        grid=(R_TILES,),
        in_specs=[
            pl.BlockSpec((TM, CONV_CH), lambda i: (i, 0)),
            pl.BlockSpec((CONV_HALO, CONV_CH), lambda i: (jnp.maximum(i * per - 1, 0), 0)),
            _const_spec((CONV_HALO, CONV_CH)),
            pl.BlockSpec((TM, D_MODEL), lambda i: (i, 0)),
            _const_spec((CONV_W, CONV_CH)),
            vec, vec, vec,
            _const_spec((CONV_CH, D_MODEL)),
            vec,
        ],
        out_specs=pl.BlockSpec((TM, D_MODEL), lambda i: (i, 0)),
        out_shape=jax.ShapeDtypeStruct((N_R, D_MODEL), F32),
        scratch_shapes=[pltpu.VMEM((CONV_HALO + TM, CONV_CH), F32), pltpu.VMEM((TM, CONV_CH), F32)],
        compiler_params=_cparams(("arbitrary",)),
        name="conv_seq",
    )(u, u, halo0, h, conv_w, conv_b, cn_g, cn_b, w_pw2, b_pw2)


def _conv_step_kernel(st_ref, u_ref, h_ref, cw_ref, cbias_ref, cg_ref, cb_ref, w2_ref, b2_ref, out_ref):
    n_st = CONV_W - 1
    for t in range(DEC_SEQ):
        acc = None
        for k in range(CONV_W):
            j = t + k
            zr = st_ref[j] if j < n_st else u_ref[j - n_st]
            term = zr * cw_ref[k:k + 1, :]
            acc = term if acc is None else acc + term
        out_ref[t] = _ln_swish_pw2(acc + cbias_ref[...], h_ref[t], cg_ref, cb_ref, w2_ref, b2_ref)


def _conv_step(state_t, u_t, h_t, conv_w, conv_b, cn_g, cn_b, w_pw2, b_pw2):
    return pl.pallas_call(
        _conv_step_kernel,
        out_shape=jax.ShapeDtypeStruct((DEC_SEQ, DEC_BATCH, D_MODEL), F32),
        compiler_params=pltpu.CompilerParams(vmem_limit_bytes=VMEM_LIMIT),
        name="conv_step",
    )(state_t, u_t, h_t, conv_w, conv_b, cn_g, cn_b, w_pw2, b_pw2)


def _router_kernel(hr_ref, hs_ref, g_ref, wr_ref, xn_ref, gate_ref, y_ref):
    i = pl.program_id(0)
    h = jnp.where(i < R_TILES, hr_ref[...], hs_ref[...])
    y_ref[...] = h
    xn = _rms(h, g_ref[...])
    xn_ref[...] = xn.astype(BF16)
    logits = jnp.dot(xn, wr_ref[...], preferred_element_type=F32, precision=lax.Precision.HIGHEST)
    lane = lax.broadcasted_iota(jnp.int32, (TM, LANES), 1)
    logits = jnp.where(lane < N_EXPERTS, logits, -jnp.inf)
    v1 = jnp.max(logits, axis=1, keepdims=True)
    i1 = jnp.min(jnp.where(logits == v1, lane, LANES), axis=1, keepdims=True)
    rest = jnp.where(lane == i1, -jnp.inf, logits)
    v2 = jnp.max(rest, axis=1, keepdims=True)
    i2 = jnp.min(jnp.where(rest == v2, lane, LANES), axis=1, keepdims=True)
    e2 = jnp.exp(v2 - v1)
    den = 1.0 + e2
    gate_ref[...] = jnp.where(lane == i1, 1.0 / den, 0.0) + jnp.where(lane == i2, e2 / den, 0.0)


def _router(h_r, h_s, g, w_router_p):
    return pl.pallas_call(
        _router_kernel,
        grid=(RS_TILES,),
        in_specs=[
            pl.BlockSpec((TM, D_MODEL), lambda i: (_r_idx(i), 0)),
            pl.BlockSpec((TM, D_MODEL), lambda i: (_sm_idx(i), 0)),
            _const_spec((1, D_MODEL)),
            _const_spec((D_MODEL, LANES)),
        ],
        out_specs=[
            pl.BlockSpec((TM, D_MODEL), lambda i: (i, 0)),
            pl.BlockSpec((TM, LANES), lambda i: (i, 0)),
            pl.BlockSpec((TM, D_MODEL), lambda i: (i, 0)),
        ],
        out_shape=[
            jax.ShapeDtypeStruct((N_RS, D_MODEL), BF16),
            jax.ShapeDtypeStruct((N_RS, LANES), F32),
            jax.ShapeDtypeStruct((N_RS, D_MODEL), F32),
        ],
        compiler_params=_cparams(("arbitrary",)),
        name="router",
    )(h_r, h_s, g, w_router_p)


def _experts_kernel(xn_ref, gate_ref, y_ref, wgu_ref, wd_ref, out_ref):
    e = pl.program_id(0)
    gu = _dot(xn_ref[...], wgu_ref[0])
    act = (_silu(gu[:, :EXPERT_FF]) * gu[:, EXPERT_FF:]).astype(BF16)
    ye = _dot(act, wd_ref[0])
    lane = lax.broadcasted_iota(jnp.int32, (TM, LANES), 1)
    ge = jnp.sum(jnp.where(lane == e, gate_ref[...], 0.0), axis=1, keepdims=True)
    out_ref[...] = y_ref[...] + ge * ye


def _experts(xn, gate, y, w_gu_x, w_down_x):
    return pl.pallas_call(
        _experts_kernel,
        grid=(N_EXPERTS, RS_TILES),
        in_specs=[
            pl.BlockSpec((TM, D_MODEL), lambda e, i: (i, 0)),
            pl.BlockSpec((TM, LANES), lambda e, i: (i, 0)),
            pl.BlockSpec((TM, D_MODEL), lambda e, i: (i, 0)),
            pl.BlockSpec((1, D_MODEL, 2 * EXPERT_FF), lambda e, i: (e, 0, 0)),
            pl.BlockSpec((1, EXPERT_FF, D_MODEL), lambda e, i: (e, 0, 0)),
        ],
        out_specs=pl.BlockSpec((TM, D_MODEL), lambda e, i: (i, 0)),
        out_shape=jax.ShapeDtypeStruct((N_RS, D_MODEL), F32),
        input_output_aliases={2: 0},
        compiler_params=_cparams(("arbitrary", "arbitrary")),
        name="experts",
    )(xn, gate, y, w_gu_x, w_down_x)


def _final_norm_kernel(y_ref, g_ref, o_ref):
    o_ref[...] = _rms(y_ref[...], g_ref[...])


def _final_norm(y, g, first_tile, n_tiles):
    return pl.pallas_call(
        _final_norm_kernel,
        grid=(n_tiles,),
        in_specs=[pl.BlockSpec((TM, D_MODEL), lambda i: (first_tile + i, 0)), _const_spec((1, D_MODEL))],
        out_specs=pl.BlockSpec((TM, D_MODEL), lambda i: (i, 0)),
        out_shape=jax.ShapeDtypeStruct((n_tiles * TM, D_MODEL), F32),
        compiler_params=_cparams(("arbitrary",)),
        name="final_norm",
    )(y, g)


def _rope_tables(past_len):
    pos = jnp.concatenate([
        jnp.arange(SEQ, dtype=F32) + float(N_META),
        jnp.tile(jnp.arange(DEC_SEQ, dtype=F32) + float(past_len), DEC_BATCH),
        jnp.arange(N_META, dtype=F32),
        jnp.zeros((TM - N_META,), F32),
    ])
    inv = ROPE_THETA ** (-jnp.arange(0, QK_ROPE, 2, dtype=F32) / QK_ROPE)
    ang = pos[:, None] * inv[None, :]
    cos, sin = jnp.cos(ang), jnp.sin(ang)
    rest = (pos.shape[0], LANES - QK_ROPE)
    return (jnp.concatenate([cos, cos, jnp.ones(rest, F32)], axis=1),
            jnp.concatenate([-sin, sin, jnp.zeros(rest, F32)], axis=1))


def kernel(x_prompt, x_sample, cache_mla, page_table, state_pool, state_conv, meta_tokens, norm_mix_e, w_in_e, pool_w, pool_scale, q_norm, kv_norm, w_uq, w_uk, w_uv, w_out_e, norm_ffn_e, w_gu_e, w_down_e, norm_mix_o, w_pw1, b_pw1, conv_w, conv_b, cnorm_g, cnorm_b, w_pw2, b_pw2, norm_ffn_o, router_w, w_gu_x, w_down_x, final_norm):
    assert x_prompt.shape == (BATCH, SEQ, D_MODEL) and x_sample.shape == (DEC_BATCH, DEC_SEQ, D_MODEL)
    assert norm_mix_e.shape[0] == 1 and norm_mix_o.shape[0] == 1
    past_len = page_table.shape[1] * PAGE_SIZE
    vec = lambda v: v.reshape(1, -1).astype(F32)

    w_in_p = jnp.pad(w_in_e[0], ((0, 0), (0, IN_W_PAD - w_in_e.shape[2]))).astype(BF16)
    uq = w_uq[0].reshape(Q_LORA, N_HEADS, QK_NOPE + QK_ROPE)
    w_uq_p = jnp.concatenate(
        [uq[..., QK_NOPE:], uq[..., :QK_NOPE], jnp.zeros((Q_LORA, N_HEADS, LANES - QK_NOPE - QK_ROPE), F32)],
        axis=-1).reshape(Q_LORA, N_HEADS * LANES).astype(BF16)
    ukt = jnp.transpose(w_uk[0], (1, 2, 0))
    w_ukt_p = jnp.pad(ukt, ((0, 0), (QK_ROPE, LANES - QK_NOPE - QK_ROPE), (0, 0))).astype(BF16)
    w_uv_h = jnp.transpose(w_uv[0], (1, 0, 2)).astype(BF16)
    hw = N_HEADS * LANES
    w_uk_p = jnp.pad(w_uk[0], ((0, 0), (0, 0), (QK_ROPE, LANES - QK_NOPE - QK_ROPE))).reshape(KV_LORA, hw).astype(BF16)
    w_uv_p = jnp.pad(w_uv[0], ((0, 0), (0, 0), (0, LANES - V_HEAD))).reshape(KV_LORA, hw).astype(BF16)
    pool_w_b = pool_w[0].astype(BF16)
    w_router_p = jnp.pad(router_w[0], ((0, 0), (0, LANES - N_EXPERTS))).astype(F32)
    cos_t, sin_t = _rope_tables(past_len)

    x_r = x_prompt.reshape(N_R, D_MODEL)
    x_sm = jnp.concatenate([x_sample.reshape(N_S, D_MODEL), meta_tokens.astype(F32),
                            jnp.zeros((TM - N_META, D_MODEL), F32)], axis=0)

    upool, q_all, k_all, v_all, newkv, qcat_sm = _inproj(
        x_r, x_sm, vec(norm_mix_e[0]), w_in_p, vec(q_norm[0]), vec(kv_norm[0]), w_uq_p, w_uk_p, w_uv_p,
        w_ukt_p, cos_t, sin_t)
    pscale = vec(pool_scale[0])
    pool_r = _pool_seq(upool, upool, pool_w_b, pscale, n_seq=BATCH, seq_len=SEQ, first_block=0,
                       halo_block=META_ROW // POOL_HALO, pos0=N_META)
    pool_m = _pool_seq(upool, jnp.zeros((POOL_HALO, POOL_WIDTH), F32), pool_w_b, pscale, n_seq=1,
                       seq_len=N_META, first_block=META_ROW // N_META, halo_block=0, pos0=0)
    upool_s = upool[N_R:N_RS].reshape(DEC_BATCH, DEC_SEQ, POOL_WIDTH)
    pool_s = _pool_step(jnp.transpose(state_pool[0], (1, 0, 2)), jnp.transpose(upool_s, (1, 0, 2)),
                        pool_w_b, pscale)
    pool_s = jnp.transpose(pool_s, (1, 0, 2)).reshape(N_S, POOL_WIDTH)
    pool_sm = jnp.concatenate([pool_s, pool_m, jnp.zeros((TM - N_META, POOL_WIDTH), BF16)], axis=0)

    meta_pad = ((0, LANES - N_META), (0, 0))
    k_meta = jnp.pad(k_all[META_ROW:META_ROW + N_META], meta_pad)
    v_meta = jnp.pad(v_all[META_ROW:META_ROW + N_META], meta_pad)
    attn_r = _attn_prompt(q_all, k_all, v_all, k_meta, v_meta)
    attn_m = _attn_meta(q_all, k_meta, v_meta)
    q_s = qcat_sm[:, :N_S].reshape(N_HEADS, DEC_BATCH, DEC_SEQ, KCAT)
    q_s = jnp.transpose(q_s, (1, 0, 2, 3)).reshape(DEC_BATCH, DEC_ROWS, KCAT)
    k_new = jnp.pad(newkv[N_R:N_RS].reshape(DEC_BATCH, DEC_SEQ, KV_LORA + QK_ROPE),
                    ((0, 0), (0, DEC_NEW_PAD - DEC_SEQ), (0, KCAT - KV_LORA - QK_ROPE))).astype(BF16)
    o_lat = _attn_step(page_table.astype(jnp.int32), q_s, k_new, jnp.transpose(cache_mla, (0, 1, 3, 2)))
    o_lat = jnp.transpose(o_lat.reshape(DEC_BATCH, N_HEADS, DEC_SEQ, KV_LORA), (0, 2, 1, 3))
    attn_s = _uv_proj(o_lat.reshape(N_S, N_HEADS * KV_LORA), w_uv_h)
    attn_sm = jnp.concatenate([attn_s, attn_m, jnp.zeros((TM - N_META, ATTN_WIDTH), BF16)], axis=0)

    h2 = _mix_ffn(x_r, x_sm, pool_r, pool_sm, attn_r, attn_sm, w_out_e[0].astype(BF16),
                  vec(norm_ffn_e[0]), w_gu_e[0].astype(BF16), w_down_e[0].astype(BF16))

    u = _glu(h2, vec(norm_mix_o[0]), w_pw1[0].astype(BF16), vec(b_pw1[0]))
    conv_args = (conv_w[0].astype(F32), vec(conv_b[0]), vec(cnorm_g[0]), vec(cnorm_b[0]),
                 w_pw2[0].astype(BF16), vec(b_pw2[0]))
    halo0 = jnp.concatenate([jnp.zeros((CONV_HALO - N_META, CONV_CH), F32),
                             u[META_ROW:META_ROW + N_META]], axis=0)
    h3_r = _conv_seq(u, halo0, h2, *conv_args)
    u_s = u[N_R:N_RS].reshape(DEC_BATCH, DEC_SEQ, CONV_CH)
    h2_s = h2[N_R:N_RS].reshape(DEC_BATCH, DEC_SEQ, D_MODEL)
    h3_s = _conv_step(jnp.transpose(state_conv[0], (1, 0, 2)), jnp.transpose(u_s, (1, 0, 2)),
                      jnp.transpose(h2_s, (1, 0, 2)), *conv_args)
    h3_s = jnp.transpose(h3_s, (1, 0, 2)).reshape(N_S, D_MODEL)

    xn, gate, y0 = _router(h3_r, h3_s, vec(norm_ffn_o[0]), w_router_p)
    y = _experts(xn, gate, y0, w_gu_x[0].astype(BF16), w_down_x[0].astype(BF16))
    fg = vec(final_norm)
    y_prompt = _final_norm(y, fg, 0, R_TILES).reshape(BATCH, SEQ, D_MODEL)
    y_sample = _final_norm(y, fg, R_TILES, S_TILES).reshape(DEC_BATCH, DEC_SEQ, D_MODEL)

    kv_w = KV_LORA + QK_ROPE
    kv_meta = jnp.broadcast_to(newkv[META_ROW:META_ROW + N_META][None], (BATCH, N_META, kv_w))
    new_mla_prompt = jnp.concatenate([kv_meta, newkv[:N_R].reshape(BATCH, SEQ, kv_w)], axis=1)[None]
    new_mla_sample = newkv[N_R:N_RS].reshape(1, DEC_BATCH, DEC_SEQ, kv_w)
    new_pool_prompt = upool[:N_R].reshape(BATCH, SEQ, POOL_WIDTH)[:, SEQ - POOL_BUF:][None]
    new_pool_sample = jnp.concatenate([state_pool[0][:, DEC_SEQ:], upool_s], axis=1)[None]
    new_conv_prompt = u[:N_R].reshape(BATCH, SEQ, CONV_CH)[:, SEQ - (CONV_W - 1):][None]
    new_conv_sample = jnp.concatenate([state_conv[0][:, DEC_SEQ:].astype(F32), u_s], axis=1)[None]
    return (y_prompt, y_sample, new_mla_prompt, new_mla_sample, new_pool_prompt, new_pool_sample,
            new_conv_prompt, new_conv_sample)
```

```python
import functools

import jax
import jax.numpy as jnp
from jax import lax
from jax.experimental import pallas as pl
from jax.experimental.pallas import tpu as pltpu

F32 = jnp.float32
BF16 = jnp.bfloat16

D_MODEL = 1024
BATCH = 16
SEQ = 2048
DEC_BATCH = 128
DEC_SEQ = 4
PAGE_SIZE = 128
N_META = 16
POOL_WINDOWS = (2, 4, 8, 16)
POOL_WIDTH = 512
POOL_GW = 128
POOL_BUF = 15
N_HEADS = 8
QK_NOPE = 64
QK_ROPE = 32
V_HEAD = 64
Q_LORA = 384
KV_LORA = 256
ROPE_THETA = 10000.0
ATTN_WIDTH = N_HEADS * V_HEAD
ATTN_SCALE = (QK_NOPE + QK_ROPE) ** -0.5
CONV_CH = D_MODEL
CONV_W = 31
D_FF = 2816
N_EXPERTS = 8
EXPERT_FF = 1408
EPS = 1e-6
NEG_INF = -1e30

LANES = 128
TM = 256
N_R = BATCH * SEQ
N_S = DEC_BATCH * DEC_SEQ
N_RS = N_R + N_S
R_TILES = N_R // TM
S_TILES = N_S // TM
RS_TILES = R_TILES + S_TILES
SM_ROWS = N_S + TM
NF = N_R + SM_ROWS
F_TILES = NF // TM
META_ROW = N_RS
SEQ_TILES = SEQ // TM
IN_W_PAD = 1280
KCAT = KV_LORA + LANES
QCAT = N_HEADS * KCAT
TQ = 256
PAGES_PER_STEP = 32
VMEM_LIMIT = 56 * 1024 * 1024


def _cparams(sem):
    return pltpu.CompilerParams(dimension_semantics=sem, vmem_limit_bytes=VMEM_LIMIT)


def _const_spec(shape):
    nd = len(shape)
    return pl.BlockSpec(shape, lambda *_: (0,) * nd)


def _rms(x, g):
    return x * lax.rsqrt(jnp.mean(x * x, axis=-1, keepdims=True) + EPS) * g


def _silu(x):
    return x * jax.nn.sigmoid(x)


def _dot(a, b):
    return jnp.dot(a, b, preferred_element_type=F32)


def _dot_nt(a, b):
    return lax.dot_general(a, b, (((1,), (1,)), ((), ())), preferred_element_type=F32)


def _r_idx(i):
    return jnp.minimum(i, R_TILES - 1)


def _sm_idx(i):
    return jnp.maximum(i - R_TILES, 0)


def _inproj_kernel(xr_ref, xsm_ref, gmix_ref, win_ref, qn_ref, kvn_ref, wuq_ref, wuk_ref, wuv_ref,
                   wukt_ref, cos_ref, sin_ref, upool_ref, q_ref, k_ref, v_ref, newkv_ref, qcat_ref):
    i = pl.program_id(0)
    x = jnp.where(i < R_TILES, xr_ref[...], xsm_ref[...])
    xn = _rms(x, gmix_ref[...]).astype(BF16)
    a = _dot(xn, win_ref[...])
    upool_ref[...] = a[:, :POOL_WIDTH]
    c0 = POOL_WIDTH
    c1 = c0 + Q_LORA
    c2 = c1 + KV_LORA
    cqn = _rms(a[:, c0:c1], qn_ref[...]).astype(BF16)
    q = _dot(cqn, wuq_ref[...])
    ckv = _rms(a[:, c1:c2], kvn_ref[...])
    cosv = cos_ref[...]
    sinv = sin_ref[...]
    lane = lax.broadcasted_iota(jnp.int32, (TM, LANES), 1)
    half = QK_ROPE // 2

    def rope(v):
        rolled = jnp.where(lane < half, pltpu.roll(v, LANES - half, 1), pltpu.roll(v, half, 1))
        return v * cosv + rolled * sinv

    kr = rope(a[:, c2:c2 + LANES])
    newkv_ref[:, :KV_LORA] = ckv
    newkv_ref[:, KV_LORA:] = kr[:, :QK_ROPE]
    ckv_b = ckv.astype(BF16)
    k_all = _dot(ckv_b, wuk_ref[...]) + jnp.concatenate([kr] * N_HEADS, axis=1)
    k_ref[...] = k_all.astype(BF16)
    v_ref[...] = _dot(ckv_b, wuv_ref[...]).astype(BF16)
    q_rot = [rope(q[:, h * LANES:(h + 1) * LANES]) for h in range(N_HEADS)]
    for h in range(N_HEADS):
        q_ref[:, h * LANES:(h + 1) * LANES] = q_rot[h].astype(BF16)

    @pl.when(i >= R_TILES)
    def _():
        for h in range(N_HEADS):
            qcat_ref[h, :, :KV_LORA] = _dot(q_rot[h].astype(BF16), wukt_ref[h]).astype(BF16)
            qcat_ref[h, :, KV_LORA:] = jnp.where(lane < QK_ROPE, q_rot[h], 0.0).astype(BF16)


def _inproj(x_r, x_sm, gmix, w_in_p, q_norm, kv_norm, w_uq_p, w_uk_p, w_uv_p, w_ukt_p, cos_t, sin_t):
    def tab_idx(i):
        return jnp.where(i < R_TILES, i % SEQ_TILES, i - R_TILES + SEQ_TILES)

    hw = N_HEADS * LANES
    row = lambda w: pl.BlockSpec((TM, w), lambda i: (i, 0))
    return pl.pallas_call(
        _inproj_kernel,
        grid=(F_TILES,),
        in_specs=[
            pl.BlockSpec((TM, D_MODEL), lambda i: (_r_idx(i), 0)),
            pl.BlockSpec((TM, D_MODEL), lambda i: (_sm_idx(i), 0)),
            _const_spec((1, D_MODEL)),
            _const_spec((D_MODEL, IN_W_PAD)),
            _const_spec((1, Q_LORA)),
            _const_spec((1, KV_LORA)),
            _const_spec((Q_LORA, hw)),
            _const_spec((KV_LORA, hw)),
            _const_spec((KV_LORA, hw)),
            _const_spec((N_HEADS, LANES, KV_LORA)),
            pl.BlockSpec((TM, LANES), lambda i: (tab_idx(i), 0)),
            pl.BlockSpec((TM, LANES), lambda i: (tab_idx(i), 0)),
        ],
        out_specs=[row(POOL_WIDTH), row(hw), row(hw), row(hw), row(KV_LORA + QK_ROPE),
                   pl.BlockSpec((N_HEADS, TM, KCAT), lambda i: (0, _sm_idx(i), 0))],
        out_shape=[
            jax.ShapeDtypeStruct((NF, POOL_WIDTH), F32),
            jax.ShapeDtypeStruct((NF, hw), BF16),
            jax.ShapeDtypeStruct((NF, hw), BF16),
            jax.ShapeDtypeStruct((NF, hw), BF16),
            jax.ShapeDtypeStruct((NF, KV_LORA + QK_ROPE), F32),
            jax.ShapeDtypeStruct((N_HEADS, SM_ROWS, KCAT), BF16),
        ],
        compiler_params=_cparams(("arbitrary",)),
        name="inproj",
    )(x_r, x_sm, gmix, w_in_p, q_norm, kv_norm, w_uq_p, w_uk_p, w_uv_p, w_ukt_p, cos_t, sin_t)


POOL_RC = 256
POOL_HALO = 16


def _pool_seq_kernel(u_ref, halo_ref, pw_ref, scale_ref, out_ref, z_ref, *, seq_len, pos0):
    z_ref[0:POOL_HALO, :] = halo_ref[...]
    z_ref[POOL_HALO:POOL_HALO + seq_len, :] = u_ref[...]
    rc = min(POOL_RC, seq_len)
    for g, w in enumerate(POOL_WINDOWS):
        sl = slice(g * POOL_GW, (g + 1) * POOL_GW)
        for r0 in range(0, seq_len, rc):
            base = POOL_HALO + r0
            cur = z_ref[base:base + rc, sl]
            acc = cur
            for k in range(1, w):
                acc = acc + z_ref[base - k:base - k + rc, sl]
            if pos0 + 1 >= w:
                mean = acc / float(w)
            else:
                pos = lax.broadcasted_iota(jnp.int32, (rc, POOL_GW), 0) + (pos0 + r0)
                mean = acc / jnp.minimum(pos + 1, w).astype(F32)
            d = (mean - cur).astype(BF16)
            o = _dot(d, pw_ref[g]) * scale_ref[:, sl]
            out_ref[r0:r0 + rc, sl] = o.astype(out_ref.dtype)


def _pool_seq(u_flat, halo, pool_w, pool_scale, *, n_seq, seq_len, first_block, halo_block, pos0):
    kern = functools.partial(_pool_seq_kernel, seq_len=seq_len, pos0=pos0)
    return pl.pallas_call(
        kern,
        grid=(n_seq,),
        in_specs=[
            pl.BlockSpec((seq_len, POOL_WIDTH), lambda b: (first_block + b, 0)),
            pl.BlockSpec((POOL_HALO, POOL_WIDTH), lambda b: (halo_block, 0)),
            _const_spec((len(POOL_WINDOWS), POOL_GW, POOL_GW)),
            _const_spec((1, POOL_WIDTH)),
        ],
        out_specs=pl.BlockSpec((seq_len, POOL_WIDTH), lambda b: (b, 0)),
        out_shape=jax.ShapeDtypeStruct((n_seq * seq_len, POOL_WIDTH), BF16),
        scratch_shapes=[pltpu.VMEM((POOL_HALO + seq_len, POOL_WIDTH), F32)],
        compiler_params=_cparams(("arbitrary",)),
        name="pool_seq",
    )(u_flat, halo, pool_w, pool_scale)


def _pool_step_kernel(st_ref, u_ref, pw_ref, scale_ref, out_ref):
    rows = [st_ref[j] for j in range(POOL_BUF)] + [u_ref[t] for t in range(DEC_SEQ)]
    for t in range(DEC_SEQ):
        cur = rows[POOL_BUF + t]
        for g, w in enumerate(POOL_WINDOWS):
            sl = slice(g * POOL_GW, (g + 1) * POOL_GW)
            acc = cur[:, sl]
            for k in range(1, w):
                acc = acc + rows[POOL_BUF + t - k][:, sl]
            d = (acc / float(w) - cur[:, sl]).astype(BF16)
            o = _dot(d, pw_ref[g]) * scale_ref[:, sl]
            out_ref[t, :, sl] = o.astype(out_ref.dtype)


def _pool_step(state_t, u_t, pool_w, pool_scale):
    return pl.pallas_call(
        _pool_step_kernel,
        out_shape=jax.ShapeDtypeStruct((DEC_SEQ, DEC_BATCH, POOL_WIDTH), BF16),
        compiler_params=pltpu.CompilerParams(vmem_limit_bytes=VMEM_LIMIT),
        name="pool_step",
    )(state_t, u_t, pool_w, pool_scale)


def _softmax_update(carry, s, v):
    m, l, acc = carry
    m_new = jnp.maximum(m, jnp.max(s, axis=1, keepdims=True))
    alpha = jnp.exp(m - m_new)
    p = jnp.exp(s - m_new)
    l = alpha * l + jnp.sum(p, axis=1, keepdims=True)
    acc = alpha * acc + _dot(p.astype(BF16), v)
    return m_new, l, acc


def _softmax_init(s, v):
    m = jnp.max(s, axis=1, keepdims=True)
    p = jnp.exp(s - m)
    return m, jnp.sum(p, axis=1, keepdims=True), _dot(p.astype(BF16), v)


def _attn_prompt_kernel(q_ref, k_ref, v_ref, km_ref, vm_ref, o_ref, mpart, mrep, lpart, acc):
    qi = pl.program_id(1)
    lane = lax.broadcasted_iota(jnp.int32, (TQ, LANES), 1)
    meta_ok = lane < N_META
    causal = (lax.broadcasted_iota(jnp.int32, (TQ, TQ), 1)
              <= lax.broadcasted_iota(jnp.int32, (TQ, TQ), 0))
    hs = [slice(h * LANES, (h + 1) * LANES) for h in range(N_HEADS)]

    def keys(j):
        return pl.ds(pl.multiple_of(j * TQ, TQ), TQ)

    def raw_scores(h, kb, mask):
        s = _dot_nt(q_ref[:, hs[h]], kb)
        return s if mask is None else jnp.where(mask, s, NEG_INF)

    def max_sweep(j, mask):
        for h in range(N_HEADS):
            s = raw_scores(h, k_ref[keys(j), hs[h]], mask)
            mpart[h] = jnp.maximum(mpart[h], jnp.maximum(s[:, :LANES], s[:, LANES:]))

    for h in range(N_HEADS):
        mpart[h] = raw_scores(h, km_ref[:, hs[h]], meta_ok)
    pl.loop(0, qi)(lambda j: max_sweep(j, None))
    max_sweep(qi, causal)
    for h in range(N_HEADS):
        m = jnp.max(mpart[h], axis=1, keepdims=True) * ATTN_SCALE
        mrep[h] = jnp.broadcast_to(m, (TQ, LANES))

    def probs(h, kb, mask):
        s = raw_scores(h, kb, mask)
        m = mrep[h]
        if s.shape[1] > LANES:
            m = jnp.concatenate([m] * (s.shape[1] // LANES), axis=1)
        return jnp.exp(s * ATTN_SCALE - m)

    def sum_sweep(j, mask):
        for h in range(N_HEADS):
            p = probs(h, k_ref[keys(j), hs[h]], mask)
            lpart[h] += p[:, :LANES] + p[:, LANES:]
            acc[h] += _dot(p.astype(BF16), v_ref[keys(j), hs[h]])

    for h in range(N_HEADS):
        p = probs(h, km_ref[:, hs[h]], meta_ok)
        lpart[h] = p
        acc[h] = _dot(p.astype(BF16), vm_ref[:, hs[h]])
    pl.loop(0, qi)(lambda j: sum_sweep(j, None))
    sum_sweep(qi, causal)
    outs = [(acc[h] / jnp.sum(lpart[h], axis=1, keepdims=True))[:, :V_HEAD] for h in range(N_HEADS)]
    o_ref[...] = jnp.concatenate(outs, axis=1).astype(o_ref.dtype)


def _attn_prompt(q_all, k_all, v_all, k_meta, v_meta):
    nq = SEQ // TQ
    hw = N_HEADS * LANES
    stat = pltpu.VMEM((N_HEADS, TQ, LANES), F32)
    return pl.pallas_call(
        _attn_prompt_kernel,
        grid=(BATCH, nq),
        in_specs=[
            pl.BlockSpec((TQ, hw), lambda b, i: (b * nq + i, 0)),
            pl.BlockSpec((SEQ, hw), lambda b, i: (b, 0)),
            pl.BlockSpec((SEQ, hw), lambda b, i: (b, 0)),
            _const_spec((LANES, hw)),
            _const_spec((LANES, hw)),
        ],
        out_specs=pl.BlockSpec((TQ, ATTN_WIDTH), lambda b, i: (b * nq + i, 0)),
        out_shape=jax.ShapeDtypeStruct((N_R, ATTN_WIDTH), BF16),
        scratch_shapes=[stat, stat, stat, stat],
        compiler_params=_cparams(("arbitrary", "arbitrary")),
        name="attn_prompt",
    )(q_all, k_all, v_all, k_meta, v_meta)


def _attn_meta_kernel(q_ref, km_ref, vm_ref, o_ref):
    causal = (lax.broadcasted_iota(jnp.int32, (N_META, LANES), 1)
              <= lax.broadcasted_iota(jnp.int32, (N_META, LANES), 0))
    outs = []
    for h in range(N_HEADS):
        sl = slice(h * LANES, (h + 1) * LANES)
        s = jnp.where(causal, _dot_nt(q_ref[:, sl], km_ref[:, sl]), NEG_INF) * ATTN_SCALE
        p = jnp.exp(s - jnp.max(s, axis=1, keepdims=True))
        o = _dot(p.astype(BF16), vm_ref[:, sl]) / jnp.sum(p, axis=1, keepdims=True)
        outs.append(o[:, :V_HEAD])
    o_ref[...] = jnp.concatenate(outs, axis=1).astype(o_ref.dtype)


def _attn_meta(q_all, k_meta, v_meta):
    hw = N_HEADS * LANES
    return pl.pallas_call(
        _attn_meta_kernel,
        grid=(1,),
        in_specs=[
            pl.BlockSpec((N_META, hw), lambda i: (META_ROW // N_META, 0)),
            _const_spec((LANES, hw)),
            _const_spec((LANES, hw)),
        ],
        out_specs=pl.BlockSpec((N_META, ATTN_WIDTH), lambda i: (0, 0)),
        out_shape=jax.ShapeDtypeStruct((N_META, ATTN_WIDTH), BF16),
        compiler_params=_cparams(("arbitrary",)),
        name="attn_meta",
    )(q_all, k_meta, v_meta)


DEC_ROWS = N_HEADS * DEC_SEQ
DEC_NEW_PAD = 16
CHUNK_KEYS = PAGES_PER_STEP * PAGE_SIZE


def _attn_step_kernel(pt_ref, q_ref, knew_ref, *rest):
    page_refs = rest[:PAGES_PER_STEP]
    o_ref = rest[PAGES_PER_STEP]
    kbuf, m_ref, l_ref, acc_ref = rest[PAGES_PER_STEP + 1:]
    c = pl.program_id(1)
    q = q_ref[0]

    @pl.when(c == 0)
    def _():
        kn = knew_ref[0]
        row = lax.broadcasted_iota(jnp.int32, (DEC_ROWS, DEC_NEW_PAD), 0)
        col = lax.broadcasted_iota(jnp.int32, (DEC_ROWS, DEC_NEW_PAD), 1)
        s = jnp.where(col <= row % DEC_SEQ, _dot_nt(q, kn) * ATTN_SCALE, NEG_INF)
        m, l, acc = _softmax_init(s, kn[:, :KV_LORA])
        m_ref[...] = m
        l_ref[...] = l
        acc_ref[...] = acc

    kv_w = KV_LORA + QK_ROPE
    kbuf[kv_w:, :] = jnp.zeros((KCAT - kv_w, CHUNK_KEYS), BF16)
    for p in range(PAGES_PER_STEP):
        kbuf[:kv_w, p * PAGE_SIZE:(p + 1) * PAGE_SIZE] = page_refs[p][0, 0].astype(BF16)
    s = _dot(q, kbuf[...]) * ATTN_SCALE
    m_old = m_ref[...]
    m = jnp.maximum(m_old, jnp.max(s, axis=1, keepdims=True))
    alpha = jnp.exp(m_old - m)
    p_un = jnp.exp(s - m)
    l = alpha * l_ref[...] + jnp.sum(p_un, axis=1, keepdims=True)
    acc = alpha * acc_ref[...] + _dot_nt(p_un.astype(BF16), kbuf[:KV_LORA, :])
    m_ref[...] = m
    l_ref[...] = l
    acc_ref[...] = acc

    @pl.when(c == pl.num_programs(1) - 1)
    def _():
        o_ref[0] = (acc / l).astype(o_ref.dtype)


def _attn_step(page_table, q_s, k_new, cache_t):
    n_pages = page_table.shape[1]
    assert n_pages % PAGES_PER_STEP == 0
    n_chunks = n_pages // PAGES_PER_STEP
    kv_w = cache_t.shape[2]

    def page_spec(p):
        return pl.BlockSpec((1, 1, kv_w, PAGE_SIZE),
                            lambda b, c, pt: (0, pt[b, c * PAGES_PER_STEP + p], 0, 0))

    grid_spec = pltpu.PrefetchScalarGridSpec(
        num_scalar_prefetch=1,
        grid=(DEC_BATCH, n_chunks),
        in_specs=[
            pl.BlockSpec((1, DEC_ROWS, KCAT), lambda b, c, pt: (b, 0, 0)),
            pl.BlockSpec((1, DEC_NEW_PAD, KCAT), lambda b, c, pt: (b, 0, 0)),
        ] + [page_spec(p) for p in range(PAGES_PER_STEP)],
        out_specs=pl.BlockSpec((1, DEC_ROWS, KV_LORA), lambda b, c, pt: (b, 0, 0)),
        scratch_shapes=[
            pltpu.VMEM((KCAT, CHUNK_KEYS), BF16),
            pltpu.VMEM((DEC_ROWS, 1), F32),
            pltpu.VMEM((DEC_ROWS, 1), F32),
            pltpu.VMEM((DEC_ROWS, KV_LORA), F32),
        ],
    )
    return pl.pallas_call(
        _attn_step_kernel,
        grid_spec=grid_spec,
        out_shape=jax.ShapeDtypeStruct((DEC_BATCH, DEC_ROWS, KV_LORA), BF16),
        compiler_params=_cparams(("arbitrary", "arbitrary")),
        name="attn_step",
    )(page_table, q_s, k_new, *([cache_t] * PAGES_PER_STEP))


def _uv_kernel(o_ref, wuv_ref, out_ref):
    outs = [_dot(o_ref[:, h * KV_LORA:(h + 1) * KV_LORA], wuv_ref[h]) for h in range(N_HEADS)]
    out_ref[...] = jnp.concatenate(outs, axis=1).astype(out_ref.dtype)


def _uv_proj(o_lat, w_uv_h):
    return pl.pallas_call(
        _uv_kernel,
        out_shape=jax.ShapeDtypeStruct((o_lat.shape[0], ATTN_WIDTH), BF16),
        compiler_params=pltpu.CompilerParams(vmem_limit_bytes=VMEM_LIMIT),
        name="uv_proj",
    )(o_lat, w_uv_h)


def _mix_ffn_kernel(xr_ref, xsm_ref, pr_ref, psm_ref, ar_ref, asm_ref, wout_ref, g_ref, wgu_ref,
                    wd_ref, h_ref):
    i = pl.program_id(0)
    is_r = i < R_TILES
    x = jnp.where(is_r, xr_ref[...], xsm_ref[...])
    mix = jnp.concatenate([jnp.where(is_r, pr_ref[...], psm_ref[...]),
                           jnp.where(is_r, ar_ref[...], asm_ref[...])], axis=1)
    h1 = x + _dot(mix, wout_ref[...])
    xn = _rms(h1, g_ref[...]).astype(BF16)
    gu = _dot(xn, wgu_ref[...])
    act = (_silu(gu[:, :D_FF]) * gu[:, D_FF:]).astype(BF16)
    h_ref[...] = h1 + _dot(act, wd_ref[...])


def _mix_ffn(x_r, x_sm, pool_r, pool_sm, attn_r, attn_sm, w_out, g, w_gu, w_down):
    r_spec = lambda w: pl.BlockSpec((TM, w), lambda i: (_r_idx(i), 0))
    sm_spec = lambda w: pl.BlockSpec((TM, w), lambda i: (_sm_idx(i), 0))
    return pl.pallas_call(
        _mix_ffn_kernel,
        grid=(F_TILES,),
        in_specs=[
            r_spec(D_MODEL), sm_spec(D_MODEL),
            r_spec(POOL_WIDTH), sm_spec(POOL_WIDTH),
            r_spec(ATTN_WIDTH), sm_spec(ATTN_WIDTH),
            _const_spec((D_MODEL, D_MODEL)),
            _const_spec((1, D_MODEL)),
            _const_spec((D_MODEL, 2 * D_FF)),
            _const_spec((D_FF, D_MODEL)),
        ],
        out_specs=pl.BlockSpec((TM, D_MODEL), lambda i: (i, 0)),
        out_shape=jax.ShapeDtypeStruct((NF, D_MODEL), F32),
        compiler_params=_cparams(("arbitrary",)),
        name="mix_ffn",
    )(x_r, x_sm, pool_r, pool_sm, attn_r, attn_sm, w_out, g, w_gu, w_down)


def _glu_kernel(h_ref, g_ref, w_ref, b_ref, u_ref):
    xn = _rms(h_ref[...], g_ref[...]).astype(BF16)
    a = _dot(xn, w_ref[...]) + b_ref[...]
    u_ref[...] = a[:, :CONV_CH] * jax.nn.sigmoid(a[:, CONV_CH:])


def _glu(h, g, w_pw1, b_pw1):
    return pl.pallas_call(
        _glu_kernel,
        grid=(F_TILES,),
        in_specs=[
            pl.BlockSpec((TM, D_MODEL), lambda i: (i, 0)),
            _const_spec((1, D_MODEL)),
            _const_spec((D_MODEL, 2 * CONV_CH)),
            _const_spec((1, 2 * CONV_CH)),
        ],
        out_specs=pl.BlockSpec((TM, CONV_CH), lambda i: (i, 0)),
        out_shape=jax.ShapeDtypeStruct((NF, CONV_CH), F32),
        compiler_params=_cparams(("arbitrary",)),
        name="glu",
    )(h, g, w_pw1, b_pw1)


CONV_HALO = 32


def _ln_swish_pw2(c, h, cg_ref, cb_ref, w2_ref, b2_ref):
    xc = c - jnp.mean(c, axis=-1, keepdims=True)
    var = jnp.mean(xc * xc, axis=-1, keepdims=True)
    n = xc * lax.rsqrt(var + EPS) * cg_ref[...] + cb_ref[...]
    return h + (_dot(_silu(n).astype(BF16), w2_ref[...]) + b2_ref[...])


def _conv_seq_kernel(u_ref, prev_ref, halo0_ref, h_ref, cw_ref, cbias_ref, cg_ref, cb_ref, w2_ref,
                     b2_ref, out_ref, z_ref, c_ref):
    i = pl.program_id(0)
    z_ref[0:CONV_HALO, :] = jnp.where(i % SEQ_TILES == 0, halo0_ref[...], prev_ref[...])
    z_ref[CONV_HALO:, :] = u_ref[...]
    off = CONV_HALO - (CONV_W - 1)
    for cc in range(CONV_CH // LANES):
        sl = slice(cc * LANES, (cc + 1) * LANES)
        acc = z_ref[off:off + TM, sl] * cw_ref[0:1, sl]
        for k in range(1, CONV_W):
            acc = acc + z_ref[off + k:off + k + TM, sl] * cw_ref[k:k + 1, sl]
        c_ref[:, sl] = acc + cbias_ref[:, sl]
    out_ref[...] = _ln_swish_pw2(c_ref[...], h_ref[...], cg_ref, cb_ref, w2_ref, b2_ref)


def _conv_seq(u, halo0, h, conv_w, conv_b, cn_g, cn_b, w_pw2, b_pw2):
    per = TM // CONV_HALO
    vec = _const_spec((1, CONV_CH))
    return pl.pallas_call(
        _conv_seq_kernel,
        grid=(R_TILES,),
        in_specs=[
            pl.BlockSpec((TM, CONV_CH), lambda i: (i, 0)),
            pl.BlockSpec((CONV_HALO, CONV_CH), lambda i: (jnp.maximum(i * per - 1, 0), 0)),
            _const_spec((CONV_HALO, CONV_CH)),
            pl.BlockSpec((TM, D_MODEL), lambda i: (i, 0)),
            _const_spec((CONV_W, CONV_CH)),
            vec, vec, vec,
            _const_spec((CONV_CH, D_MODEL)),
            vec,
        ],
        out_specs=pl.BlockSpec((TM, D_MODEL), lambda i: (i, 0)),
        out_shape=jax.ShapeDtypeStruct((N_R, D_MODEL), F32),
        scratch_shapes=[pltpu.VMEM((CONV_HALO + TM, CONV_CH), F32), pltpu.VMEM((TM, CONV_CH), F32)],
        compiler_params=_cparams(("arbitrary",)),
        name="conv_seq",
    )(u, u, halo0, h, conv_w, conv_b, cn_g, cn_b, w_pw2, b_pw2)


def _conv_step_kernel(st_ref, u_ref, h_ref, cw_ref, cbias_ref, cg_ref, cb_ref, w2_ref, b2_ref, out_ref):
    n_st = CONV_W - 1
    for t in range(DEC_SEQ):
        acc = None
        for k in range(CONV_W):
            j = t + k
            zr = st_ref[j] if j < n_st else u_ref[j - n_st]
            term = zr * cw_ref[k:k + 1, :]
            acc = term if acc is None else acc + term
        out_ref[t] = _ln_swish_pw2(acc + cbias_ref[...], h_ref[t], cg_ref, cb_ref, w2_ref, b2_ref)


def _conv_step(state_t, u_t, h_t, conv_w, conv_b, cn_g, cn_b, w_pw2, b_pw2):
    return pl.pallas_call(
        _conv_step_kernel,
        out_shape=jax.ShapeDtypeStruct((DEC_SEQ, DEC_BATCH, D_MODEL), F32),
        compiler_params=pltpu.CompilerParams(vmem_limit_bytes=VMEM_LIMIT),
        name="conv_step",
    )(state_t, u_t, h_t, conv_w, conv_b, cn_g, cn_b, w_pw2, b_pw2)


TG = 512
NT_MAX = -(-(2 * N_RS + N_EXPERTS * (TG - 1)) // TG)
NS = NT_MAX * TG
REC_I1, REC_I2, REC_G1, REC_G2, REC_P1, REC_P2 = range(6)


def _lane_col(x, lane, k):
    return jnp.sum(jnp.where(lane == k, x, 0.0), axis=1, keepdims=True)


def _router_kernel(hr_ref, hs_ref, g_ref, wr_ref, rec_ref, cnt_ref, base_ref):
    i = pl.program_id(0)

    @pl.when(i == 0)
    def _():
        base_ref[...] = jnp.zeros_like(base_ref)

    h = jnp.where(i < R_TILES, hr_ref[...], hs_ref[...])
    xn = _rms(h, g_ref[...])
    logits = jnp.dot(xn, wr_ref[...], preferred_element_type=F32, precision=lax.Precision.HIGHEST)
    lane = lax.broadcasted_iota(jnp.int32, (TM, LANES), 1)
    logits = jnp.where(lane < N_EXPERTS, logits, -jnp.inf)
    v1 = jnp.max(logits, axis=1, keepdims=True)
    i1 = jnp.min(jnp.where(logits == v1, lane, LANES), axis=1, keepdims=True)
    rest = jnp.where(lane == i1, -jnp.inf, logits)
    v2 = jnp.max(rest, axis=1, keepdims=True)
    i2 = jnp.min(jnp.where(rest == v2, lane, LANES), axis=1, keepdims=True)
    e2 = jnp.exp(v2 - v1)
    den = 1.0 + e2
    chosen = jnp.where((lane == i1) | (lane == i2), 1.0, 0.0)
    tri = jnp.where(lax.broadcasted_iota(jnp.int32, (TM, TM), 1) < lax.broadcasted_iota(jnp.int32, (TM, TM), 0),
                    1.0, 0.0).astype(BF16)
    rank = _dot(tri, chosen.astype(BF16)) + base_ref[...]
    base_ref[...] = base_ref[...] + jnp.sum(chosen, axis=0, keepdims=True)
    cnt_ref[...] = base_ref[...]
    rec = jnp.zeros((TM, LANES), F32)
    for k, val in ((REC_I1, i1.astype(F32)), (REC_I2, i2.astype(F32)), (REC_G1, 1.0 / den), (REC_G2, e2 / den),
                   (REC_P1, _lane_col(rank, lane, i1)), (REC_P2, _lane_col(rank, lane, i2))):
        rec = jnp.where(lane == k, val, rec)
    rec_ref[...] = rec


def _router(h_r, h_s, g, w_router_p):
    return pl.pallas_call(
        _router_kernel,
        grid=(RS_TILES,),
        in_specs=[
            pl.BlockSpec((TM, D_MODEL), lambda i: (_r_idx(i), 0)),
            pl.BlockSpec((TM, D_MODEL), lambda i: (_sm_idx(i), 0)),
            _const_spec((1, D_MODEL)),
            _const_spec((D_MODEL, LANES)),
        ],
        out_specs=[pl.BlockSpec((TM, LANES), lambda i: (i, 0)), _const_spec((1, LANES))],
        out_shape=[jax.ShapeDtypeStruct((N_RS, LANES), F32), jax.ShapeDtypeStruct((1, LANES), F32)],
        scratch_shapes=[pltpu.VMEM((1, LANES), F32)],
        compiler_params=_cparams(("arbitrary",)),
        name="router",
    )(h_r, h_s, g, w_router_p)


def _row_copy(src, s_row, dst, d_row, sem):
    return pltpu.make_async_copy(src.at[pl.ds(s_row, 1), :], dst.at[pl.ds(d_row, 1), :], sem)


def _wait_rows(src, dst, sem, n_bufs):
    for _ in range(n_bufs):
        pltpu.make_async_copy(src, dst, sem).wait()


def _dispatch_kernel(pad_ref, slots_ref, hr_ref, hs_ref, g_ref, xs_hbm, xbuf, zbuf, sem, zsem):
    i = pl.program_id(0)
    h = jnp.where(i < R_TILES, hr_ref[...], hs_ref[...])
    xbuf[...] = _rms(h, g_ref[...])

    def issue(t, carry):
        _row_copy(xbuf, t, xs_hbm, slots_ref[0, 0, 2 * t], sem).start()
        _row_copy(xbuf, t, xs_hbm, slots_ref[0, 0, 2 * t + 1], sem).start()
        return carry

    lax.fori_loop(0, TM, issue, 0, unroll=8)

    @pl.when(i == 0)
    def _():
        zbuf[...] = jnp.zeros_like(zbuf)
        for e in range(N_EXPERTS):
            def zero_row(r, carry, e=e):
                _row_copy(zbuf, 0, xs_hbm, pad_ref[0, e] + r, zsem).start()
                return carry
            lax.fori_loop(0, pad_ref[1, e], zero_row, 0)
        for e in range(N_EXPERTS):
            def wait_row(r, carry):
                _row_copy(zbuf, 0, xs_hbm, 0, zsem).wait()
                return carry
            lax.fori_loop(0, pad_ref[1, e], wait_row, 0)

    _wait_rows(xbuf, xs_hbm.at[pl.ds(0, TM), :], sem, 2)


def _dispatch(pad_info, slots, h_r, h_s, g):
    grid_spec = pltpu.PrefetchScalarGridSpec(
        num_scalar_prefetch=1,
        grid=(RS_TILES,),
        in_specs=[
            pl.BlockSpec((1, 1, 2 * TM), lambda i, pad: (i, 0, 0), memory_space=pltpu.SMEM),
            pl.BlockSpec((TM, D_MODEL), lambda i, pad: (_r_idx(i), 0)),
            pl.BlockSpec((TM, D_MODEL), lambda i, pad: (_sm_idx(i), 0)),
            pl.BlockSpec((1, D_MODEL), lambda i, pad: (0, 0)),
        ],
        out_specs=pl.BlockSpec(memory_space=pl.ANY),
        scratch_shapes=[
            pltpu.VMEM((TM, D_MODEL), F32),
            pltpu.VMEM((8, D_MODEL), F32),
            pltpu.SemaphoreType.DMA(()),
            pltpu.SemaphoreType.DMA(()),
        ],
    )
    return pl.pallas_call(
        _dispatch_kernel,
        grid_spec=grid_spec,
        out_shape=jax.ShapeDtypeStruct((NS, D_MODEL), F32),
        compiler_params=_cparams(("arbitrary",)),
        name="dispatch",
    )(pad_info, slots, h_r, h_s, g)


def _experts_kernel(te_ref, nu_ref, x_ref, wgu_ref, wd_ref, y_ref):
    j = pl.program_id(0)

    @pl.when(j < nu_ref[0])
    def _():
        gu = _dot(x_ref[...].astype(BF16), wgu_ref[0])
        act = (_silu(gu[:, :EXPERT_FF]) * gu[:, EXPERT_FF:]).astype(BF16)
        y_ref[...] = _dot(act, wd_ref[0])

    @pl.when(j >= nu_ref[0])
    def _():
        y_ref[...] = jnp.zeros_like(y_ref)


def _experts(tile_expert, n_used, xs, w_gu_x, w_down_x):
    grid_spec = pltpu.PrefetchScalarGridSpec(
        num_scalar_prefetch=2,
        grid=(NT_MAX,),
        in_specs=[
            pl.BlockSpec((TG, D_MODEL), lambda j, te, nu: (jnp.minimum(j, nu[0] - 1), 0)),
            pl.BlockSpec((1, D_MODEL, 2 * EXPERT_FF), lambda j, te, nu: (te[j], 0, 0)),
            pl.BlockSpec((1, EXPERT_FF, D_MODEL), lambda j, te, nu: (te[j], 0, 0)),
        ],
        out_specs=pl.BlockSpec((TG, D_MODEL), lambda j, te, nu: (j, 0)),
    )
    return pl.pallas_call(
        _experts_kernel,
        grid_spec=grid_spec,
        out_shape=jax.ShapeDtypeStruct((NS, D_MODEL), F32),
        compiler_params=_cparams(("arbitrary",)),
        name="experts",
    )(tile_expert, n_used, xs, w_gu_x, w_down_x)


def _combine_kernel(slots_ref, hr_ref, hs_ref, rec_ref, fg_ref, ys_hbm, outr_ref, outs_ref, r1buf, r2buf, sem):
    i = pl.program_id(0)

    def issue(t, carry):
        _row_copy(ys_hbm, slots_ref[0, 0, 2 * t], r1buf, t, sem).start()
        _row_copy(ys_hbm, slots_ref[0, 0, 2 * t + 1], r2buf, t, sem).start()
        return carry

    lax.fori_loop(0, TM, issue, 0, unroll=8)
    h = jnp.where(i < R_TILES, hr_ref[...], hs_ref[...])
    lane = lax.broadcasted_iota(jnp.int32, (TM, LANES), 1)
    rec = rec_ref[...]
    g1 = _lane_col(rec, lane, REC_G1)
    g2 = _lane_col(rec, lane, REC_G2)
    _wait_rows(ys_hbm.at[pl.ds(0, TM), :], r1buf, sem, 2)
    res = _rms(h + g1 * r1buf[...] + g2 * r2buf[...], fg_ref[...])

    @pl.when(i < R_TILES)
    def _():
        outr_ref[...] = res

    @pl.when(i >= R_TILES)
    def _():
        outs_ref[...] = res


def _combine(slots, h_r, h_s, rec, fg, ys):
    return pl.pallas_call(
        _combine_kernel,
        grid=(RS_TILES,),
        in_specs=[
            pl.BlockSpec((1, 1, 2 * TM), lambda i: (i, 0, 0), memory_space=pltpu.SMEM),
            pl.BlockSpec((TM, D_MODEL), lambda i: (_r_idx(i), 0)),
            pl.BlockSpec((TM, D_MODEL), lambda i: (_sm_idx(i), 0)),
            pl.BlockSpec((TM, LANES), lambda i: (i, 0)),
            _const_spec((1, D_MODEL)),
            pl.BlockSpec(memory_space=pl.ANY),
        ],
        out_specs=[
            pl.BlockSpec((TM, D_MODEL), lambda i: (_r_idx(i), 0)),
            pl.BlockSpec((TM, D_MODEL), lambda i: (_sm_idx(i), 0)),
        ],
        out_shape=[jax.ShapeDtypeStruct((N_R, D_MODEL), F32), jax.ShapeDtypeStruct((N_S, D_MODEL), F32)],
        scratch_shapes=[
            pltpu.VMEM((TM, D_MODEL), F32),
            pltpu.VMEM((TM, D_MODEL), F32),
            pltpu.SemaphoreType.DMA(()),
        ],
        compiler_params=_cparams(("arbitrary",)),
        name="combine",
    )(slots, h_r, h_s, rec, fg, ys)


def _moe(h_r, h_s, g, w_router_p, w_gu_x, w_down_x, fg):
    rec, counts = _router(h_r, h_s, g, w_router_p)
    cnt = counts[0, :N_EXPERTS].astype(jnp.int32)
    tiles = (cnt + (TG - 1)) // TG
    eidx = jnp.arange(N_EXPERTS, dtype=jnp.int32)
    tile_end = jnp.sum(jnp.where(eidx[None, :] <= eidx[:, None], tiles[None, :], 0), axis=1)
    slot_off = (tile_end - tiles) * TG
    ids = rec[:, REC_I1:REC_I2 + 1].astype(jnp.int32)
    pos = rec[:, REC_P1:REC_P2 + 1].astype(jnp.int32)
    tok_off = jnp.sum(jnp.where(ids[..., None] == eidx, slot_off, 0), axis=-1)
    slots = (tok_off + pos).reshape(RS_TILES, 1, 2 * TM)
    tile_idx = jnp.arange(NT_MAX, dtype=jnp.int32)
    tile_expert = jnp.minimum(jnp.sum((tile_idx[:, None] >= tile_end[None, :]).astype(jnp.int32), axis=1),
                              N_EXPERTS - 1)
    n_used = tile_end[-1:]
    pad_info = jnp.stack([slot_off + cnt, tiles * TG - cnt]).astype(jnp.int32)
    xs = _dispatch(pad_info, slots, h_r, h_s, g)
    ys = _experts(tile_expert, n_used, xs, w_gu_x, w_down_x)
    return _combine(slots, h_r, h_s, rec, fg, ys)


def _rope_tables(past_len):
    pos = jnp.concatenate([
        jnp.arange(SEQ, dtype=F32) + float(N_META),
        jnp.tile(jnp.arange(DEC_SEQ, dtype=F32) + float(past_len), DEC_BATCH),
        jnp.arange(N_META, dtype=F32),
        jnp.zeros((TM - N_META,), F32),
    ])
    inv = ROPE_THETA ** (-jnp.arange(0, QK_ROPE, 2, dtype=F32) / QK_ROPE)
    ang = pos[:, None] * inv[None, :]
    cos, sin = jnp.cos(ang), jnp.sin(ang)
    rest = (pos.shape[0], LANES - QK_ROPE)
    return (jnp.concatenate([cos, cos, jnp.ones(rest, F32)], axis=1),
            jnp.concatenate([-sin, sin, jnp.zeros(rest, F32)], axis=1))


def kernel(x_prompt, x_sample, cache_mla, page_table, state_pool, state_conv, meta_tokens, norm_mix_e, w_in_e, pool_w, pool_scale, q_norm, kv_norm, w_uq, w_uk, w_uv, w_out_e, norm_ffn_e, w_gu_e, w_down_e, norm_mix_o, w_pw1, b_pw1, conv_w, conv_b, cnorm_g, cnorm_b, w_pw2, b_pw2, norm_ffn_o, router_w, w_gu_x, w_down_x, final_norm):
    assert x_prompt.shape == (BATCH, SEQ, D_MODEL) and x_sample.shape == (DEC_BATCH, DEC_SEQ, D_MODEL)
    assert norm_mix_e.shape[0] == 1 and norm_mix_o.shape[0] == 1
    past_len = page_table.shape[1] * PAGE_SIZE
    vec = lambda v: v.reshape(1, -1).astype(F32)

    w_in_p = jnp.pad(w_in_e[0], ((0, 0), (0, IN_W_PAD - w_in_e.shape[2]))).astype(BF16)
    uq = w_uq[0].reshape(Q_LORA, N_HEADS, QK_NOPE + QK_ROPE)
    w_uq_p = jnp.concatenate(
        [uq[..., QK_NOPE:], uq[..., :QK_NOPE], jnp.zeros((Q_LORA, N_HEADS, LANES - QK_NOPE - QK_ROPE), F32)],
        axis=-1).reshape(Q_LORA, N_HEADS * LANES).astype(BF16)
    ukt = jnp.transpose(w_uk[0], (1, 2, 0))
    w_ukt_p = jnp.pad(ukt, ((0, 0), (QK_ROPE, LANES - QK_NOPE - QK_ROPE), (0, 0))).astype(BF16)
    w_uv_h = jnp.transpose(w_uv[0], (1, 0, 2)).astype(BF16)
    hw = N_HEADS * LANES
    w_uk_p = jnp.pad(w_uk[0], ((0, 0), (0, 0), (QK_ROPE, LANES - QK_NOPE - QK_ROPE))).reshape(KV_LORA, hw).astype(BF16)
    w_uv_p = jnp.pad(w_uv[0], ((0, 0), (0, 0), (0, LANES - V_HEAD))).reshape(KV_LORA, hw).astype(BF16)
    pool_w_b = pool_w[0].astype(BF16)
    w_router_p = jnp.pad(router_w[0], ((0, 0), (0, LANES - N_EXPERTS))).astype(F32)
    cos_t, sin_t = _rope_tables(past_len)

    x_r = x_prompt.reshape(N_R, D_MODEL)
    x_sm = jnp.concatenate([x_sample.reshape(N_S, D_MODEL), meta_tokens.astype(F32),
                            jnp.zeros((TM - N_META, D_MODEL), F32)], axis=0)

    upool, q_all, k_all, v_all, newkv, qcat_sm = _inproj(
        x_r, x_sm, vec(norm_mix_e[0]), w_in_p, vec(q_norm[0]), vec(kv_norm[0]), w_uq_p, w_uk_p, w_uv_p,
        w_ukt_p, cos_t, sin_t)
    pscale = vec(pool_scale[0])
    pool_r = _pool_seq(upool, upool, pool_w_b, pscale, n_seq=BATCH, seq_len=SEQ, first_block=0,
                       halo_block=META_ROW // POOL_HALO, pos0=N_META)
    pool_m = _pool_seq(upool, jnp.zeros((POOL_HALO, POOL_WIDTH), F32), pool_w_b, pscale, n_seq=1,
                       seq_len=N_META, first_block=META_ROW // N_META, halo_block=0, pos0=0)
    upool_s = upool[N_R:N_RS].reshape(DEC_BATCH, DEC_SEQ, POOL_WIDTH)
    pool_s = _pool_step(jnp.transpose(state_pool[0], (1, 0, 2)), jnp.transpose(upool_s, (1, 0, 2)),
                        pool_w_b, pscale)
    pool_s = jnp.transpose(pool_s, (1, 0, 2)).reshape(N_S, POOL_WIDTH)
    pool_sm = jnp.concatenate([pool_s, pool_m, jnp.zeros((TM - N_META, POOL_WIDTH), BF16)], axis=0)

    meta_pad = ((0, LANES - N_META), (0, 0))
    k_meta = jnp.pad(k_all[META_ROW:META_ROW + N_META], meta_pad)
    v_meta = jnp.pad(v_all[META_ROW:META_ROW + N_META], meta_pad)
    attn_r = _attn_prompt(q_all, k_all, v_all, k_meta, v_meta)
    attn_m = _attn_meta(q_all, k_meta, v_meta)
    q_s = qcat_sm[:, :N_S].reshape(N_HEADS, DEC_BATCH, DEC_SEQ, KCAT)
    q_s = jnp.transpose(q_s, (1, 0, 2, 3)).reshape(DEC_BATCH, DEC_ROWS, KCAT)
    k_new = jnp.pad(newkv[N_R:N_RS].reshape(DEC_BATCH, DEC_SEQ, KV_LORA + QK_ROPE),
                    ((0, 0), (0, DEC_NEW_PAD - DEC_SEQ), (0, KCAT - KV_LORA - QK_ROPE))).astype(BF16)
    o_lat = _attn_step(page_table.astype(jnp.int32), q_s, k_new, jnp.transpose(cache_mla, (0, 1, 3, 2)))
    o_lat = jnp.transpose(o_lat.reshape(DEC_BATCH, N_HEADS, DEC_SEQ, KV_LORA), (0, 2, 1, 3))
    attn_s = _uv_proj(o_lat.reshape(N_S, N_HEADS * KV_LORA), w_uv_h)
    attn_sm = jnp.concatenate([attn_s, attn_m, jnp.zeros((TM - N_META, ATTN_WIDTH), BF16)], axis=0)

    h2 = _mix_ffn(x_r, x_sm, pool_r, pool_sm, attn_r, attn_sm, w_out_e[0].astype(BF16),
                  vec(norm_ffn_e[0]), w_gu_e[0].astype(BF16), w_down_e[0].astype(BF16))

    u = _glu(h2, vec(norm_mix_o[0]), w_pw1[0].astype(BF16), vec(b_pw1[0]))
    conv_args = (conv_w[0].astype(F32), vec(conv_b[0]), vec(cnorm_g[0]), vec(cnorm_b[0]),
                 w_pw2[0].astype(BF16), vec(b_pw2[0]))
    halo0 = jnp.concatenate([jnp.zeros((CONV_HALO - N_META, CONV_CH), F32),
                             u[META_ROW:META_ROW + N_META]], axis=0)
    h3_r = _conv_seq(u, halo0, h2, *conv_args)
    u_s = u[N_R:N_RS].reshape(DEC_BATCH, DEC_SEQ, CONV_CH)
    h2_s = h2[N_R:N_RS].reshape(DEC_BATCH, DEC_SEQ, D_MODEL)
    h3_s = _conv_step(jnp.transpose(state_conv[0], (1, 0, 2)), jnp.transpose(u_s, (1, 0, 2)),
                      jnp.transpose(h2_s, (1, 0, 2)), *conv_args)
    h3_s = jnp.transpose(h3_s, (1, 0, 2)).reshape(N_S, D_MODEL)

    y_r, y_s = _moe(h3_r, h3_s, vec(norm_ffn_o[0]), w_router_p, w_gu_x[0].astype(BF16),
                    w_down_x[0].astype(BF16), vec(final_norm))
    y_prompt = y_r.reshape(BATCH, SEQ, D_MODEL)
    y_sample = y_s.reshape(DEC_BATCH, DEC_SEQ, D_MODEL)

    kv_w = KV_LORA + QK_ROPE
    kv_meta = jnp.broadcast_to(newkv[META_ROW:META_ROW + N_META][None], (BATCH, N_META, kv_w))
    new_mla_prompt = jnp.concatenate([kv_meta, newkv[:N_R].reshape(BATCH, SEQ, kv_w)], axis=1)[None]
    new_mla_sample = newkv[N_R:N_RS].reshape(1, DEC_BATCH, DEC_SEQ, kv_w)
    new_pool_prompt = upool[:N_R].reshape(BATCH, SEQ, POOL_WIDTH)[:, SEQ - POOL_BUF:][None]
    new_pool_sample = jnp.concatenate([state_pool[0][:, DEC_SEQ:], upool_s], axis=1)[None]
    new_conv_prompt = u[:N_R].reshape(BATCH, SEQ, CONV_CH)[:, SEQ - (CONV_W - 1):][None]
    new_conv_sample = jnp.concatenate([state_conv[0][:, DEC_SEQ:].astype(F32), u_s], axis=1)[None]
    return (y_prompt, y_sample, new_mla_prompt, new_mla_sample, new_pool_prompt, new_pool_sample,
            new_conv_prompt, new_conv_sample)
```

```python
import functools

import jax
import jax.numpy as jnp
from jax import lax
from jax.experimental import pallas as pl
from jax.experimental.pallas import tpu as pltpu

F32 = jnp.float32
BF16 = jnp.bfloat16

D_MODEL = 1024
BATCH = 16
SEQ = 2048
DEC_BATCH = 128
DEC_SEQ = 4
PAGE_SIZE = 128
N_META = 16
POOL_WINDOWS = (2, 4, 8, 16)
POOL_WIDTH = 512
POOL_GW = 128
POOL_BUF = 15
N_HEADS = 8
QK_NOPE = 64
QK_ROPE = 32
V_HEAD = 64
Q_LORA = 384
KV_LORA = 256
ROPE_THETA = 10000.0
ATTN_WIDTH = N_HEADS * V_HEAD
ATTN_SCALE = (QK_NOPE + QK_ROPE) ** -0.5
CONV_CH = D_MODEL
CONV_W = 31
D_FF = 2816
N_EXPERTS = 8
EXPERT_FF = 1408
EPS = 1e-6
NEG_INF = -1e30

LANES = 128
TM = 256
N_R = BATCH * SEQ
N_S = DEC_BATCH * DEC_SEQ
N_RS = N_R + N_S
R_TILES = N_R // TM
S_TILES = N_S // TM
RS_TILES = R_TILES + S_TILES
SM_ROWS = N_S + TM
NF = N_R + SM_ROWS
F_TILES = NF // TM
META_ROW = N_RS
SEQ_TILES = SEQ // TM
IN_W_PAD = 1280
KCAT = KV_LORA + LANES
QCAT = N_HEADS * KCAT
TQ = 256
PAGES_PER_STEP = 32
VMEM_LIMIT = 56 * 1024 * 1024


def _cparams(sem):
    return pltpu.CompilerParams(dimension_semantics=sem, vmem_limit_bytes=VMEM_LIMIT)


def _const_spec(shape):
    nd = len(shape)
    return pl.BlockSpec(shape, lambda *_: (0,) * nd)


def _rms(x, g):
    return x * lax.rsqrt(jnp.mean(x * x, axis=-1, keepdims=True) + EPS) * g


def _silu(x):
    return x * jax.nn.sigmoid(x)


def _dot(a, b):
    return jnp.dot(a, b, preferred_element_type=F32)


def _dot_nt(a, b):
    return lax.dot_general(a, b, (((1,), (1,)), ((), ())), preferred_element_type=F32)


def _r_idx(i):
    return jnp.minimum(i, R_TILES - 1)


def _sm_idx(i):
    return jnp.maximum(i - R_TILES, 0)


def _inproj_kernel(xr_ref, xsm_ref, gmix_ref, win_ref, qn_ref, kvn_ref, wuq_ref, wuk_ref, wuv_ref,
                   wukt_ref, cos_ref, sin_ref, upool_ref, q_ref, k_ref, v_ref, newkv_ref, qcat_ref):
    i = pl.program_id(0)
    x = jnp.where(i < R_TILES, xr_ref[...], xsm_ref[...])
    xn = _rms(x, gmix_ref[...]).astype(BF16)
    a = _dot(xn, win_ref[...])
    upool_ref[...] = a[:, :POOL_WIDTH]
    c0 = POOL_WIDTH
    c1 = c0 + Q_LORA
    c2 = c1 + KV_LORA
    cqn = _rms(a[:, c0:c1], qn_ref[...]).astype(BF16)
    q = _dot(cqn, wuq_ref[...])
    ckv = _rms(a[:, c1:c2], kvn_ref[...])
    cosv = cos_ref[...]
    sinv = sin_ref[...]
    lane = lax.broadcasted_iota(jnp.int32, (TM, LANES), 1)
    half = QK_ROPE // 2

    def rope(v):
        rolled = jnp.where(lane < half, pltpu.roll(v, LANES - half, 1), pltpu.roll(v, half, 1))
        return v * cosv + rolled * sinv

    kr = rope(a[:, c2:c2 + LANES])
    newkv_ref[:, :KV_LORA] = ckv
    newkv_ref[:, KV_LORA:] = kr[:, :QK_ROPE]
    ckv_b = ckv.astype(BF16)
    k_all = _dot(ckv_b, wuk_ref[...]) + jnp.concatenate([kr] * N_HEADS, axis=1)
    k_ref[...] = k_all.astype(BF16)
    v_ref[...] = _dot(ckv_b, wuv_ref[...]).astype(BF16)
    q_rot = [rope(q[:, h * LANES:(h + 1) * LANES]) for h in range(N_HEADS)]
    for h in range(N_HEADS):
        q_ref[:, h * LANES:(h + 1) * LANES] = q_rot[h].astype(BF16)

    @pl.when(i >= R_TILES)
    def _():
        for h in range(N_HEADS):
            qcat_ref[h, :, :KV_LORA] = _dot(q_rot[h].astype(BF16), wukt_ref[h]).astype(BF16)
            qcat_ref[h, :, KV_LORA:] = jnp.where(lane < QK_ROPE, q_rot[h], 0.0).astype(BF16)


def _inproj(x_r, x_sm, gmix, w_in_p, q_norm, kv_norm, w_uq_p, w_uk_p, w_uv_p, w_ukt_p, cos_t, sin_t):
    def tab_idx(i):
        return jnp.where(i < R_TILES, i % SEQ_TILES, i - R_TILES + SEQ_TILES)

    hw = N_HEADS * LANES
    row = lambda w: pl.BlockSpec((TM, w), lambda i: (i, 0))
    return pl.pallas_call(
        _inproj_kernel,
        grid=(F_TILES,),
        in_specs=[
            pl.BlockSpec((TM, D_MODEL), lambda i: (_r_idx(i), 0)),
            pl.BlockSpec((TM, D_MODEL), lambda i: (_sm_idx(i), 0)),
            _const_spec((1, D_MODEL)),
            _const_spec((D_MODEL, IN_W_PAD)),
            _const_spec((1, Q_LORA)),
            _const_spec((1, KV_LORA)),
            _const_spec((Q_LORA, hw)),
            _const_spec((KV_LORA, hw)),
            _const_spec((KV_LORA, hw)),
            _const_spec((N_HEADS, LANES, KV_LORA)),
            pl.BlockSpec((TM, LANES), lambda i: (tab_idx(i), 0)),
            pl.BlockSpec((TM, LANES), lambda i: (tab_idx(i), 0)),
        ],
        out_specs=[row(POOL_WIDTH), row(hw), row(hw), row(hw), row(KV_LORA + QK_ROPE),
                   pl.BlockSpec((N_HEADS, TM, KCAT), lambda i: (0, _sm_idx(i), 0))],
        out_shape=[
            jax.ShapeDtypeStruct((NF, POOL_WIDTH), F32),
            jax.ShapeDtypeStruct((NF, hw), BF16),
            jax.ShapeDtypeStruct((NF, hw), BF16),
            jax.ShapeDtypeStruct((NF, hw), BF16),
            jax.ShapeDtypeStruct((NF, KV_LORA + QK_ROPE), F32),
            jax.ShapeDtypeStruct((N_HEADS, SM_ROWS, KCAT), BF16),
        ],
        compiler_params=_cparams(("arbitrary",)),
        name="inproj",
    )(x_r, x_sm, gmix, w_in_p, q_norm, kv_norm, w_uq_p, w_uk_p, w_uv_p, w_ukt_p, cos_t, sin_t)


POOL_RC = 256
POOL_HALO = 16


def _pool_seq_kernel(u_ref, halo_ref, pw_ref, scale_ref, out_ref, z_ref, *, seq_len, pos0):
    z_ref[0:POOL_HALO, :] = halo_ref[...]
    z_ref[POOL_HALO:POOL_HALO + seq_len, :] = u_ref[...]
    rc = min(POOL_RC, seq_len)
    for g, w in enumerate(POOL_WINDOWS):
        sl = slice(g * POOL_GW, (g + 1) * POOL_GW)
        for r0 in range(0, seq_len, rc):
            base = POOL_HALO + r0
            cur = z_ref[base:base + rc, sl]
            acc = cur
            for k in range(1, w):
                acc = acc + z_ref[base - k:base - k + rc, sl]
            if pos0 + 1 >= w:
                mean = acc / float(w)
            else:
                pos = lax.broadcasted_iota(jnp.int32, (rc, POOL_GW), 0) + (pos0 + r0)
                mean = acc / jnp.minimum(pos + 1, w).astype(F32)
            d = (mean - cur).astype(BF16)
            o = _dot(d, pw_ref[g]) * scale_ref[:, sl]
            out_ref[r0:r0 + rc, sl] = o.astype(out_ref.dtype)


def _pool_seq(u_flat, halo, pool_w, pool_scale, *, n_seq, seq_len, first_block, halo_block, pos0):
    kern = functools.partial(_pool_seq_kernel, seq_len=seq_len, pos0=pos0)
    return pl.pallas_call(
        kern,
        grid=(n_seq,),
        in_specs=[
            pl.BlockSpec((seq_len, POOL_WIDTH), lambda b: (first_block + b, 0)),
            pl.BlockSpec((POOL_HALO, POOL_WIDTH), lambda b: (halo_block, 0)),
            _const_spec((len(POOL_WINDOWS), POOL_GW, POOL_GW)),
            _const_spec((1, POOL_WIDTH)),
        ],
        out_specs=pl.BlockSpec((seq_len, POOL_WIDTH), lambda b: (b, 0)),
        out_shape=jax.ShapeDtypeStruct((n_seq * seq_len, POOL_WIDTH), BF16),
        scratch_shapes=[pltpu.VMEM((POOL_HALO + seq_len, POOL_WIDTH), F32)],
        compiler_params=_cparams(("arbitrary",)),
        name="pool_seq",
    )(u_flat, halo, pool_w, pool_scale)


def _pool_step_kernel(st_ref, u_ref, pw_ref, scale_ref, out_ref):
    rows = [st_ref[j] for j in range(POOL_BUF)] + [u_ref[t] for t in range(DEC_SEQ)]
    for t in range(DEC_SEQ):
        cur = rows[POOL_BUF + t]
        for g, w in enumerate(POOL_WINDOWS):
            sl = slice(g * POOL_GW, (g + 1) * POOL_GW)
            acc = cur[:, sl]
            for k in range(1, w):
                acc = acc + rows[POOL_BUF + t - k][:, sl]
            d = (acc / float(w) - cur[:, sl]).astype(BF16)
            o = _dot(d, pw_ref[g]) * scale_ref[:, sl]
            out_ref[t, :, sl] = o.astype(out_ref.dtype)


def _pool_step(state_t, u_t, pool_w, pool_scale):
    return pl.pallas_call(
        _pool_step_kernel,
        out_shape=jax.ShapeDtypeStruct((DEC_SEQ, DEC_BATCH, POOL_WIDTH), BF16),
        compiler_params=pltpu.CompilerParams(vmem_limit_bytes=VMEM_LIMIT),
        name="pool_step",
    )(state_t, u_t, pool_w, pool_scale)


def _softmax_update(carry, s, v):
    m, l, acc = carry
    m_new = jnp.maximum(m, jnp.max(s, axis=1, keepdims=True))
    alpha = jnp.exp(m - m_new)
    p = jnp.exp(s - m_new)
    l = alpha * l + jnp.sum(p, axis=1, keepdims=True)
    acc = alpha * acc + _dot(p.astype(BF16), v)
    return m_new, l, acc


def _softmax_init(s, v):
    m = jnp.max(s, axis=1, keepdims=True)
    p = jnp.exp(s - m)
    return m, jnp.sum(p, axis=1, keepdims=True), _dot(p.astype(BF16), v)


def _attn_prompt_kernel(q_ref, k_ref, v_ref, km_ref, vm_ref, o_ref, mpart, mrep, lpart, acc):
    qi = pl.program_id(1)
    lane = lax.broadcasted_iota(jnp.int32, (TQ, LANES), 1)
    meta_ok = lane < N_META
    causal = (lax.broadcasted_iota(jnp.int32, (TQ, TQ), 1)
              <= lax.broadcasted_iota(jnp.int32, (TQ, TQ), 0))
    hs = [slice(h * LANES, (h + 1) * LANES) for h in range(N_HEADS)]

    def keys(j):
        return pl.ds(pl.multiple_of(j * TQ, TQ), TQ)

    def raw_scores(h, kb, mask):
        s = _dot_nt(q_ref[:, hs[h]], kb)
        return s if mask is None else jnp.where(mask, s, NEG_INF)

    def max_sweep(j, mask):
        for h in range(N_HEADS):
            s = raw_scores(h, k_ref[keys(j), hs[h]], mask)
            mpart[h] = jnp.maximum(mpart[h], jnp.maximum(s[:, :LANES], s[:, LANES:]))

    for h in range(N_HEADS):
        mpart[h] = raw_scores(h, km_ref[:, hs[h]], meta_ok)
    pl.loop(0, qi)(lambda j: max_sweep(j, None))
    max_sweep(qi, causal)
    for h in range(N_HEADS):
        m = jnp.max(mpart[h], axis=1, keepdims=True) * ATTN_SCALE
        mrep[h] = jnp.broadcast_to(m, (TQ, LANES))

    def probs(h, kb, mask):
        s = raw_scores(h, kb, mask)
        m = mrep[h]
        if s.shape[1] > LANES:
            m = jnp.concatenate([m] * (s.shape[1] // LANES), axis=1)
        return jnp.exp(s * ATTN_SCALE - m)

    def sum_sweep(j, mask):
        for h in range(N_HEADS):
            p = probs(h, k_ref[keys(j), hs[h]], mask)
            lpart[h] += p[:, :LANES] + p[:, LANES:]
            acc[h] += _dot(p.astype(BF16), v_ref[keys(j), hs[h]])

    for h in range(N_HEADS):
        p = probs(h, km_ref[:, hs[h]], meta_ok)
        lpart[h] = p
        acc[h] = _dot(p.astype(BF16), vm_ref[:, hs[h]])
    pl.loop(0, qi)(lambda j: sum_sweep(j, None))
    sum_sweep(qi, causal)
    outs = [(acc[h] / jnp.sum(lpart[h], axis=1, keepdims=True))[:, :V_HEAD] for h in range(N_HEADS)]
    o_ref[...] = jnp.concatenate(outs, axis=1).astype(o_ref.dtype)


def _attn_prompt(q_all, k_all, v_all, k_meta, v_meta):
    nq = SEQ // TQ
    hw = N_HEADS * LANES
    stat = pltpu.VMEM((N_HEADS, TQ, LANES), F32)
    return pl.pallas_call(
        _attn_prompt_kernel,
        grid=(BATCH, nq),
        in_specs=[
            pl.BlockSpec((TQ, hw), lambda b, i: (b * nq + i, 0)),
            pl.BlockSpec((SEQ, hw), lambda b, i: (b, 0)),
            pl.BlockSpec((SEQ, hw), lambda b, i: (b, 0)),
            _const_spec((LANES, hw)),
            _const_spec((LANES, hw)),
        ],
        out_specs=pl.BlockSpec((TQ, ATTN_WIDTH), lambda b, i: (b * nq + i, 0)),
        out_shape=jax.ShapeDtypeStruct((N_R, ATTN_WIDTH), BF16),
        scratch_shapes=[stat, stat, stat, stat],
        compiler_params=_cparams(("arbitrary", "arbitrary")),
        name="attn_prompt",
    )(q_all, k_all, v_all, k_meta, v_meta)


def _attn_meta_kernel(q_ref, km_ref, vm_ref, o_ref):
    causal = (lax.broadcasted_iota(jnp.int32, (N_META, LANES), 1)
              <= lax.broadcasted_iota(jnp.int32, (N_META, LANES), 0))
    outs = []
    for h in range(N_HEADS):
        sl = slice(h * LANES, (h + 1) * LANES)
        s = jnp.where(causal, _dot_nt(q_ref[:, sl], km_ref[:, sl]), NEG_INF) * ATTN_SCALE
        p = jnp.exp(s - jnp.max(s, axis=1, keepdims=True))
        o = _dot(p.astype(BF16), vm_ref[:, sl]) / jnp.sum(p, axis=1, keepdims=True)
        outs.append(o[:, :V_HEAD])
    o_ref[...] = jnp.concatenate(outs, axis=1).astype(o_ref.dtype)


def _attn_meta(q_all, k_meta, v_meta):
    hw = N_HEADS * LANES
    return pl.pallas_call(
        _attn_meta_kernel,
        grid=(1,),
        in_specs=[
            pl.BlockSpec((N_META, hw), lambda i: (META_ROW // N_META, 0)),
            _const_spec((LANES, hw)),
            _const_spec((LANES, hw)),
        ],
        out_specs=pl.BlockSpec((N_META, ATTN_WIDTH), lambda i: (0, 0)),
        out_shape=jax.ShapeDtypeStruct((N_META, ATTN_WIDTH), BF16),
        compiler_params=_cparams(("arbitrary",)),
        name="attn_meta",
    )(q_all, k_meta, v_meta)


DEC_ROWS = N_HEADS * DEC_SEQ
DEC_NEW_PAD = 16
CHUNK_KEYS = PAGES_PER_STEP * PAGE_SIZE
SUB_KEYS = CHUNK_KEYS // 2


def _attn_step_kernel(pt_ref, q_ref, knew_ref, cache_hbm, o_ref, pbuf, kbuf, m_ref, l_ref, acc_ref, sems):
    b = pl.program_id(0)
    c = pl.program_id(1)
    n_chunks = pl.num_programs(1)
    step = b * n_chunks + c
    slot = step % 2

    def page_copy(bb, cc, p, sl):
        page = pt_ref[bb, cc * PAGES_PER_STEP + p]
        return pltpu.make_async_copy(cache_hbm.at[0, page], pbuf.at[sl, p], sems.at[sl])

    @pl.when(step == 0)
    def _():
        for p in range(PAGES_PER_STEP):
            page_copy(0, 0, p, 0).start()

    last_c = c == n_chunks - 1
    nb = jnp.where(last_c, b + 1, b)
    nc = jnp.where(last_c, 0, c + 1)

    @pl.when(nb < pl.num_programs(0))
    def _():
        for p in range(PAGES_PER_STEP):
            page_copy(nb, nc, p, 1 - slot).start()

    q = q_ref[0]

    @pl.when(c == 0)
    def _():
        kn = knew_ref[0]
        row = lax.broadcasted_iota(jnp.int32, (DEC_ROWS, DEC_NEW_PAD), 0)
        col = lax.broadcasted_iota(jnp.int32, (DEC_ROWS, DEC_NEW_PAD), 1)
        s = jnp.where(col <= row % DEC_SEQ, _dot_nt(q, kn) * ATTN_SCALE, NEG_INF)
        m, l, acc = _softmax_init(s, kn[:, :KV_LORA])
        m_ref[...] = m
        l_ref[...] = l
        acc_ref[...] = acc

    for p in range(PAGES_PER_STEP):
        page_copy(b, c, p, slot).wait()
    kv_w = KV_LORA + QK_ROPE
    kbuf[kv_w:, :] = jnp.zeros((KCAT - kv_w, CHUNK_KEYS), BF16)
    for p in range(PAGES_PER_STEP):
        kbuf[:kv_w, p * PAGE_SIZE:(p + 1) * PAGE_SIZE] = pbuf[slot, p].astype(BF16)
    m, l, acc = m_ref[...], l_ref[...], acc_ref[...]
    for h in range(CHUNK_KEYS // SUB_KEYS):
        ks = kbuf[:, h * SUB_KEYS:(h + 1) * SUB_KEYS]
        s = _dot(q, ks) * ATTN_SCALE
        m_new = jnp.maximum(m, jnp.max(s, axis=1, keepdims=True))
        alpha = jnp.exp(m - m_new)
        p_un = jnp.exp(s - m_new)
        l = alpha * l + jnp.sum(p_un, axis=1, keepdims=True)
        acc = alpha * acc + _dot_nt(p_un.astype(BF16), ks[:KV_LORA, :])
        m = m_new
    m_ref[...] = m
    l_ref[...] = l
    acc_ref[...] = acc

    @pl.when(last_c)
    def _():
        o_ref[0] = (acc / l).astype(o_ref.dtype)


def _attn_step(page_table, q_s, k_new, cache_t):
    n_pages = page_table.shape[1]
    assert n_pages % PAGES_PER_STEP == 0
    n_chunks = n_pages // PAGES_PER_STEP
    kv_w = cache_t.shape[2]
    grid_spec = pltpu.PrefetchScalarGridSpec(
        num_scalar_prefetch=1,
        grid=(DEC_BATCH, n_chunks),
        in_specs=[
            pl.BlockSpec((1, DEC_ROWS, KCAT), lambda b, c, pt: (b, 0, 0)),
            pl.BlockSpec((1, DEC_NEW_PAD, KCAT), lambda b, c, pt: (b, 0, 0)),
            pl.BlockSpec(memory_space=pl.ANY),
        ],
        out_specs=pl.BlockSpec((1, DEC_ROWS, KV_LORA), lambda b, c, pt: (b, 0, 0)),
        scratch_shapes=[
            pltpu.VMEM((2, PAGES_PER_STEP, kv_w, PAGE_SIZE), F32),
            pltpu.VMEM((KCAT, CHUNK_KEYS), BF16),
            pltpu.VMEM((DEC_ROWS, 1), F32),
            pltpu.VMEM((DEC_ROWS, 1), F32),
            pltpu.VMEM((DEC_ROWS, KV_LORA), F32),
            pltpu.SemaphoreType.DMA((2,)),
        ],
    )
    return pl.pallas_call(
        _attn_step_kernel,
        grid_spec=grid_spec,
        out_shape=jax.ShapeDtypeStruct((DEC_BATCH, DEC_ROWS, KV_LORA), BF16),
        compiler_params=_cparams(("arbitrary", "arbitrary")),
        name="attn_step",
    )(page_table, q_s, k_new, cache_t)


def _uv_kernel(o_ref, wuv_ref, out_ref):
    outs = [_dot(o_ref[:, h * KV_LORA:(h + 1) * KV_LORA], wuv_ref[h]) for h in range(N_HEADS)]
    out_ref[...] = jnp.concatenate(outs, axis=1).astype(out_ref.dtype)


def _uv_proj(o_lat, w_uv_h):
    return pl.pallas_call(
        _uv_kernel,
        out_shape=jax.ShapeDtypeStruct((o_lat.shape[0], ATTN_WIDTH), BF16),
        compiler_params=pltpu.CompilerParams(vmem_limit_bytes=VMEM_LIMIT),
        name="uv_proj",
    )(o_lat, w_uv_h)


def _mix_ffn_kernel(xr_ref, xsm_ref, pr_ref, psm_ref, ar_ref, asm_ref, wout_ref, g_ref, wgu_ref,
                    wd_ref, h_ref):
    i = pl.program_id(0)
    is_r = i < R_TILES
    x = jnp.where(is_r, xr_ref[...], xsm_ref[...])
    mix = jnp.concatenate([jnp.where(is_r, pr_ref[...], psm_ref[...]),
                           jnp.where(is_r, ar_ref[...], asm_ref[...])], axis=1)
    h1 = x + _dot(mix, wout_ref[...])
    xn = _rms(h1, g_ref[...]).astype(BF16)
    gu = _dot(xn, wgu_ref[...])
    act = (_silu(gu[:, :D_FF]) * gu[:, D_FF:]).astype(BF16)
    h_ref[...] = h1 + _dot(act, wd_ref[...])


def _mix_ffn(x_r, x_sm, pool_r, pool_sm, attn_r, attn_sm, w_out, g, w_gu, w_down):
    r_spec = lambda w: pl.BlockSpec((TM, w), lambda i: (_r_idx(i), 0))
    sm_spec = lambda w: pl.BlockSpec((TM, w), lambda i: (_sm_idx(i), 0))
    return pl.pallas_call(
        _mix_ffn_kernel,
        grid=(F_TILES,),
        in_specs=[
            r_spec(D_MODEL), sm_spec(D_MODEL),
            r_spec(POOL_WIDTH), sm_spec(POOL_WIDTH),
            r_spec(ATTN_WIDTH), sm_spec(ATTN_WIDTH),
            _const_spec((D_MODEL, D_MODEL)),
            _const_spec((1, D_MODEL)),
            _const_spec((D_MODEL, 2 * D_FF)),
            _const_spec((D_FF, D_MODEL)),
        ],
        out_specs=pl.BlockSpec((TM, D_MODEL), lambda i: (i, 0)),
        out_shape=jax.ShapeDtypeStruct((NF, D_MODEL), F32),
        compiler_params=_cparams(("arbitrary",)),
        name="mix_ffn",
    )(x_r, x_sm, pool_r, pool_sm, attn_r, attn_sm, w_out, g, w_gu, w_down)


def _glu_kernel(h_ref, g_ref, w_ref, b_ref, u_ref):
    xn = _rms(h_ref[...], g_ref[...]).astype(BF16)
    a = _dot(xn, w_ref[...]) + b_ref[...]
    u_ref[...] = a[:, :CONV_CH] * jax.nn.sigmoid(a[:, CONV_CH:])


def _glu(h, g, w_pw1, b_pw1):
    return pl.pallas_call(
        _glu_kernel,
        grid=(F_TILES,),
        in_specs=[
            pl.BlockSpec((TM, D_MODEL), lambda i: (i, 0)),
            _const_spec((1, D_MODEL)),
            _const_spec((D_MODEL, 2 * CONV_CH)),
            _const_spec((1, 2 * CONV_CH)),
        ],
        out_specs=pl.BlockSpec((TM, CONV_CH), lambda i: (i, 0)),
        out_shape=jax.ShapeDtypeStruct((NF, CONV_CH), F32),
        compiler_params=_cparams(("arbitrary",)),
        name="glu",
    )(h, g, w_pw1, b_pw1)


CONV_HALO = 32


def _ln_swish_pw2(c, h, cg_ref, cb_ref, w2_ref, b2_ref):
    xc = c - jnp.mean(c, axis=-1, keepdims=True)
    var = jnp.mean(xc * xc, axis=-1, keepdims=True)
    n = xc * lax.rsqrt(var + EPS) * cg_ref[...] + cb_ref[...]
    return h + (_dot(_silu(n).astype(BF16), w2_ref[...]) + b2_ref[...])


def _conv_seq_kernel(u_ref, prev_ref, halo0_ref, h_ref, cw_ref, cbias_ref, cg_ref, cb_ref, w2_ref,
                     b2_ref, out_ref, z_ref, c_ref):
    i = pl.program_id(0)
    z_ref[0:CONV_HALO, :] = jnp.where(i % SEQ_TILES == 0, halo0_ref[...], prev_ref[...])
    z_ref[CONV_HALO:, :] = u_ref[...]
    off = CONV_HALO - (CONV_W - 1)
    sub = 8
    for cc in range(CONV_CH // LANES):
        sl = slice(cc * LANES, (cc + 1) * LANES)
        acc = None
        for ph in range(sub):
            rows = TM + (sub if ph else 0)
            part = None
            for a in range((CONV_W - 1 + off) // sub + 1):
                k = sub * a + ph - off
                if 0 <= k < CONV_W:
                    term = z_ref[sub * a:sub * a + rows, sl] * cw_ref[k:k + 1, sl]
                    part = term if part is None else part + term
            part = part[ph:ph + TM]
            acc = part if acc is None else acc + part
        c_ref[:, sl] = acc + cbias_ref[:, sl]
    out_ref[...] = _ln_swish_pw2(c_ref[...], h_ref[...], cg_ref, cb_ref, w2_ref, b2_ref)


def _conv_seq(u, halo0, h, conv_w, conv_b, cn_g, cn_b, w_pw2, b_pw2):
    per = TM // CONV_HALO
    vec = _const_spec((1, CONV_CH))
    return pl.pallas_call(
        _conv_seq_kernel,
        grid=(R_TILES,),
        in_specs=[
            pl.BlockSpec((TM, CONV_CH), lambda i: (i, 0)),
            pl.BlockSpec((CONV_HALO, CONV_CH), lambda i: (jnp.maximum(i * per - 1, 0), 0)),
            _const_spec((CONV_HALO, CONV_CH)),
            pl.BlockSpec((TM, D_MODEL), lambda i: (i, 0)),
            _const_spec((CONV_W, CONV_CH)),
            vec, vec, vec,
            _const_spec((CONV_CH, D_MODEL)),
            vec,
        ],
        out_specs=pl.BlockSpec((TM, D_MODEL), lambda i: (i, 0)),
        out_shape=jax.ShapeDtypeStruct((N_R, D_MODEL), F32),
        scratch_shapes=[pltpu.VMEM((CONV_HALO + TM, CONV_CH), F32), pltpu.VMEM((TM, CONV_CH), F32)],
        compiler_params=_cparams(("arbitrary",)),
        name="conv_seq",
    )(u, u, halo0, h, conv_w, conv_b, cn_g, cn_b, w_pw2, b_pw2)


def _conv_step_kernel(st_ref, u_ref, h_ref, cw_ref, cbias_ref, cg_ref, cb_ref, w2_ref, b2_ref, out_ref):
    n_st = CONV_W - 1
    for t in range(DEC_SEQ):
        acc = None
        for k in range(CONV_W):
            j = t + k
            zr = st_ref[j] if j < n_st else u_ref[j - n_st]
            term = zr * cw_ref[k:k + 1, :]
            acc = term if acc is None else acc + term
        out_ref[t] = _ln_swish_pw2(acc + cbias_ref[...], h_ref[t], cg_ref, cb_ref, w2_ref, b2_ref)


def _conv_step(state_t, u_t, h_t, conv_w, conv_b, cn_g, cn_b, w_pw2, b_pw2):
    return pl.pallas_call(
        _conv_step_kernel,
        out_shape=jax.ShapeDtypeStruct((DEC_SEQ, DEC_BATCH, D_MODEL), F32),
        compiler_params=pltpu.CompilerParams(vmem_limit_bytes=VMEM_LIMIT),
        name="conv_step",
    )(state_t, u_t, h_t, conv_w, conv_b, cn_g, cn_b, w_pw2, b_pw2)


TG = 512
NT_MAX = -(-(2 * N_RS + N_EXPERTS * (TG - 1)) // TG)
NS = NT_MAX * TG
REC_I1, REC_I2, REC_G1, REC_G2, REC_P1, REC_P2 = range(6)


def _lane_col(x, lane, k):
    return jnp.sum(jnp.where(lane == k, x, 0.0), axis=1, keepdims=True)


def _router_kernel(hr_ref, hs_ref, g_ref, wr_ref, rec_ref, cnt_ref, base_ref):
    i = pl.program_id(0)

    @pl.when(i == 0)
    def _():
        base_ref[...] = jnp.zeros_like(base_ref)

    h = jnp.where(i < R_TILES, hr_ref[...], hs_ref[...])
    xn = _rms(h, g_ref[...])
    logits = jnp.dot(xn, wr_ref[...], preferred_element_type=F32, precision=lax.Precision.HIGHEST)
    lane = lax.broadcasted_iota(jnp.int32, (TM, LANES), 1)
    logits = jnp.where(lane < N_EXPERTS, logits, -jnp.inf)
    v1 = jnp.max(logits, axis=1, keepdims=True)
    i1 = jnp.min(jnp.where(logits == v1, lane, LANES), axis=1, keepdims=True)
    rest = jnp.where(lane == i1, -jnp.inf, logits)
    v2 = jnp.max(rest, axis=1, keepdims=True)
    i2 = jnp.min(jnp.where(rest == v2, lane, LANES), axis=1, keepdims=True)
    e2 = jnp.exp(v2 - v1)
    den = 1.0 + e2
    chosen = jnp.where((lane == i1) | (lane == i2), 1.0, 0.0)
    tri = jnp.where(lax.broadcasted_iota(jnp.int32, (TM, TM), 1) < lax.broadcasted_iota(jnp.int32, (TM, TM), 0),
                    1.0, 0.0).astype(BF16)
    rank = _dot(tri, chosen.astype(BF16)) + base_ref[...]
    base_ref[...] = base_ref[...] + jnp.sum(chosen, axis=0, keepdims=True)
    cnt_ref[...] = base_ref[...]
    rec = jnp.zeros((TM, LANES), F32)
    for k, val in ((REC_I1, i1.astype(F32)), (REC_I2, i2.astype(F32)), (REC_G1, 1.0 / den), (REC_G2, e2 / den),
                   (REC_P1, _lane_col(rank, lane, i1)), (REC_P2, _lane_col(rank, lane, i2))):
        rec = jnp.where(lane == k, val, rec)
    rec_ref[...] = rec


def _router(h_r, h_s, g, w_router_p):
    return pl.pallas_call(
        _router_kernel,
        grid=(RS_TILES,),
        in_specs=[
            pl.BlockSpec((TM, D_MODEL), lambda i: (_r_idx(i), 0)),
            pl.BlockSpec((TM, D_MODEL), lambda i: (_sm_idx(i), 0)),
            _const_spec((1, D_MODEL)),
            _const_spec((D_MODEL, LANES)),
        ],
        out_specs=[pl.BlockSpec((TM, LANES), lambda i: (i, 0)), _const_spec((1, LANES))],
        out_shape=[jax.ShapeDtypeStruct((N_RS, LANES), F32), jax.ShapeDtypeStruct((1, LANES), F32)],
        scratch_shapes=[pltpu.VMEM((1, LANES), F32)],
        compiler_params=_cparams(("arbitrary",)),
        name="router",
    )(h_r, h_s, g, w_router_p)


def _row_copy(src, s_row, dst, d_row, sem):
    return pltpu.make_async_copy(src.at[pl.ds(s_row, 1), :], dst.at[pl.ds(d_row, 1), :], sem)


def _wait_rows(src, dst, sem, n_bufs):
    for _ in range(n_bufs):
        pltpu.make_async_copy(src, dst, sem).wait()


def _dispatch_kernel(pad_ref, slots_ref, hr_ref, hs_ref, g_ref, xs_hbm, xbuf, zbuf, sem, zsem):
    i = pl.program_id(0)
    h = jnp.where(i < R_TILES, hr_ref[...], hs_ref[...])
    xbuf[...] = _rms(h, g_ref[...])

    def issue(t, carry):
        _row_copy(xbuf, t, xs_hbm, slots_ref[0, 0, 2 * t], sem).start()
        _row_copy(xbuf, t, xs_hbm, slots_ref[0, 0, 2 * t + 1], sem).start()
        return carry

    lax.fori_loop(0, TM, issue, 0, unroll=8)

    @pl.when(i == 0)
    def _():
        zbuf[...] = jnp.zeros_like(zbuf)
        for e in range(N_EXPERTS):
            def zero_row(r, carry, e=e):
                _row_copy(zbuf, 0, xs_hbm, pad_ref[0, e] + r, zsem).start()
                return carry
            lax.fori_loop(0, pad_ref[1, e], zero_row, 0)
        for e in range(N_EXPERTS):
            def wait_row(r, carry):
                _row_copy(zbuf, 0, xs_hbm, 0, zsem).wait()
                return carry
            lax.fori_loop(0, pad_ref[1, e], wait_row, 0)

        def tail_copy(j):
            return pltpu.make_async_copy(zbuf, xs_hbm.at[pl.ds(pl.multiple_of(j * TM, TM), TM), :], zsem)

        first_tail = pad_ref[2, 0] * (TG // TM)

        def zero_tail(j, carry):
            tail_copy(j).start()
            return carry

        def wait_tail(j, carry):
            tail_copy(j).wait()
            return carry

        lax.fori_loop(first_tail, NS // TM, zero_tail, 0)
        lax.fori_loop(first_tail, NS // TM, wait_tail, 0)

    _wait_rows(xbuf, xs_hbm.at[pl.ds(0, TM), :], sem, 2)


def _dispatch(pad_info, slots, h_r, h_s, g):
    grid_spec = pltpu.PrefetchScalarGridSpec(
        num_scalar_prefetch=1,
        grid=(RS_TILES,),
        in_specs=[
            pl.BlockSpec((1, 1, 2 * TM), lambda i, pad: (i, 0, 0), memory_space=pltpu.SMEM),
            pl.BlockSpec((TM, D_MODEL), lambda i, pad: (_r_idx(i), 0)),
            pl.BlockSpec((TM, D_MODEL), lambda i, pad: (_sm_idx(i), 0)),
            pl.BlockSpec((1, D_MODEL), lambda i, pad: (0, 0)),
        ],
        out_specs=pl.BlockSpec(memory_space=pl.ANY),
        scratch_shapes=[
            pltpu.VMEM((TM, D_MODEL), F32),
            pltpu.VMEM((TM, D_MODEL), F32),
            pltpu.SemaphoreType.DMA(()),
            pltpu.SemaphoreType.DMA(()),
        ],
    )
    return pl.pallas_call(
        _dispatch_kernel,
        grid_spec=grid_spec,
        out_shape=jax.ShapeDtypeStruct((NS, D_MODEL), F32),
        compiler_params=_cparams(("arbitrary",)),
        name="dispatch",
    )(pad_info, slots, h_r, h_s, g)


def _experts_kernel(te_ref, nu_ref, x_ref, wgu_ref, wd_ref, y_ref):
    j = pl.program_id(0)

    @pl.when(j < nu_ref[0])
    def _():
        gu = _dot(x_ref[...].astype(BF16), wgu_ref[0])
        act = (_silu(gu[:, :EXPERT_FF]) * gu[:, EXPERT_FF:]).astype(BF16)
        y_ref[...] = _dot(act, wd_ref[0])

    @pl.when(j >= nu_ref[0])
    def _():
        y_ref[...] = jnp.zeros_like(y_ref)


def _experts(tile_expert, n_used, xs, w_gu_x, w_down_x):
    grid_spec = pltpu.PrefetchScalarGridSpec(
        num_scalar_prefetch=2,
        grid=(NT_MAX,),
        in_specs=[
            pl.BlockSpec((TG, D_MODEL), lambda j, te, nu: (jnp.minimum(j, nu[0] - 1), 0)),
            pl.BlockSpec((1, D_MODEL, 2 * EXPERT_FF), lambda j, te, nu: (te[j], 0, 0)),
            pl.BlockSpec((1, EXPERT_FF, D_MODEL), lambda j, te, nu: (te[j], 0, 0)),
        ],
        out_specs=pl.BlockSpec((TG, D_MODEL), lambda j, te, nu: (j, 0)),
    )
    return pl.pallas_call(
        _experts_kernel,
        grid_spec=grid_spec,
        out_shape=jax.ShapeDtypeStruct((NS, D_MODEL), F32),
        compiler_params=_cparams(("arbitrary",)),
        name="experts",
    )(tile_expert, n_used, xs, w_gu_x, w_down_x)


def _combine_kernel(slots_ref, hr_ref, hs_ref, rec_ref, fg_ref, ys_hbm, outr_ref, outs_ref, r1buf, r2buf, sem):
    i = pl.program_id(0)

    def issue(t, carry):
        _row_copy(ys_hbm, slots_ref[0, 0, 2 * t], r1buf, t, sem).start()
        _row_copy(ys_hbm, slots_ref[0, 0, 2 * t + 1], r2buf, t, sem).start()
        return carry

    lax.fori_loop(0, TM, issue, 0, unroll=8)
    h = jnp.where(i < R_TILES, hr_ref[...], hs_ref[...])
    lane = lax.broadcasted_iota(jnp.int32, (TM, LANES), 1)
    rec = rec_ref[...]
    g1 = _lane_col(rec, lane, REC_G1)
    g2 = _lane_col(rec, lane, REC_G2)
    _wait_rows(ys_hbm.at[pl.ds(0, TM), :], r1buf, sem, 2)
    res = _rms(h + g1 * r1buf[...] + g2 * r2buf[...], fg_ref[...])

    @pl.when(i < R_TILES)
    def _():
        outr_ref[...] = res

    @pl.when(i >= R_TILES)
    def _():
        outs_ref[...] = res


def _combine(slots, h_r, h_s, rec, fg, ys):
    return pl.pallas_call(
        _combine_kernel,
        grid=(RS_TILES,),
        in_specs=[
            pl.BlockSpec((1, 1, 2 * TM), lambda i: (i, 0, 0), memory_space=pltpu.SMEM),
            pl.BlockSpec((TM, D_MODEL), lambda i: (_r_idx(i), 0)),
            pl.BlockSpec((TM, D_MODEL), lambda i: (_sm_idx(i), 0)),
            pl.BlockSpec((TM, LANES), lambda i: (i, 0)),
            _const_spec((1, D_MODEL)),
            pl.BlockSpec(memory_space=pl.ANY),
        ],
        out_specs=[
            pl.BlockSpec((TM, D_MODEL), lambda i: (_r_idx(i), 0)),
            pl.BlockSpec((TM, D_MODEL), lambda i: (_sm_idx(i), 0)),
        ],
        out_shape=[jax.ShapeDtypeStruct((N_R, D_MODEL), F32), jax.ShapeDtypeStruct((N_S, D_MODEL), F32)],
        scratch_shapes=[
            pltpu.VMEM((TM, D_MODEL), F32),
            pltpu.VMEM((TM, D_MODEL), F32),
            pltpu.SemaphoreType.DMA(()),
        ],
        compiler_params=_cparams(("arbitrary",)),
        name="combine",
    )(slots, h_r, h_s, rec, fg, ys)


def _moe(h_r, h_s, g, w_router_p, w_gu_x, w_down_x, fg):
    rec, counts = _router(h_r, h_s, g, w_router_p)
    cnt = counts[0, :N_EXPERTS].astype(jnp.int32)
    tiles = (cnt + (TG - 1)) // TG
    eidx = jnp.arange(N_EXPERTS, dtype=jnp.int32)
    tile_end = jnp.sum(jnp.where(eidx[None, :] <= eidx[:, None], tiles[None, :], 0), axis=1)
    slot_off = (tile_end - tiles) * TG
    ids = rec[:, REC_I1:REC_I2 + 1].astype(jnp.int32)
    pos = rec[:, REC_P1:REC_P2 + 1].astype(jnp.int32)
    tok_off = jnp.sum(jnp.where(ids[..., None] == eidx, slot_off, 0), axis=-1)
    slots = (tok_off + pos).reshape(RS_TILES, 1, 2 * TM)
    tile_idx = jnp.arange(NT_MAX, dtype=jnp.int32)
    tile_expert = jnp.minimum(jnp.sum((tile_idx[:, None] >= tile_end[None, :]).astype(jnp.int32), axis=1),
                              N_EXPERTS - 1)
    n_used = tile_end[-1:]
    pad_info = jnp.stack([slot_off + cnt, tiles * TG - cnt, jnp.broadcast_to(n_used, (N_EXPERTS,))])
    xs = _dispatch(pad_info, slots, h_r, h_s, g)
    ys = _experts(tile_expert, n_used, xs, w_gu_x, w_down_x)
    return _combine(slots, h_r, h_s, rec, fg, ys)


def _rope_tables(past_len):
    pos = jnp.concatenate([
        jnp.arange(SEQ, dtype=F32) + float(N_META),
        jnp.tile(jnp.arange(DEC_SEQ, dtype=F32) + float(past_len), DEC_BATCH),
        jnp.arange(N_META, dtype=F32),
        jnp.zeros((TM - N_META,), F32),
    ])
    inv = ROPE_THETA ** (-jnp.arange(0, QK_ROPE, 2, dtype=F32) / QK_ROPE)
    ang = pos[:, None] * inv[None, :]
    cos, sin = jnp.cos(ang), jnp.sin(ang)
    rest = (pos.shape[0], LANES - QK_ROPE)
    return (jnp.concatenate([cos, cos, jnp.ones(rest, F32)], axis=1),
            jnp.concatenate([-sin, sin, jnp.zeros(rest, F32)], axis=1))


def kernel(x_prompt, x_sample, cache_mla, page_table, state_pool, state_conv, meta_tokens, norm_mix_e, w_in_e, pool_w, pool_scale, q_norm, kv_norm, w_uq, w_uk, w_uv, w_out_e, norm_ffn_e, w_gu_e, w_down_e, norm_mix_o, w_pw1, b_pw1, conv_w, conv_b, cnorm_g, cnorm_b, w_pw2, b_pw2, norm_ffn_o, router_w, w_gu_x, w_down_x, final_norm):
    assert x_prompt.shape == (BATCH, SEQ, D_MODEL) and x_sample.shape == (DEC_BATCH, DEC_SEQ, D_MODEL)
    assert norm_mix_e.shape[0] == 1 and norm_mix_o.shape[0] == 1
    past_len = page_table.shape[1] * PAGE_SIZE
    vec = lambda v: v.reshape(1, -1).astype(F32)

    w_in_p = jnp.pad(w_in_e[0], ((0, 0), (0, IN_W_PAD - w_in_e.shape[2]))).astype(BF16)
    uq = w_uq[0].reshape(Q_LORA, N_HEADS, QK_NOPE + QK_ROPE)
    w_uq_p = jnp.concatenate(
        [uq[..., QK_NOPE:], uq[..., :QK_NOPE], jnp.zeros((Q_LORA, N_HEADS, LANES - QK_NOPE - QK_ROPE), F32)],
        axis=-1).reshape(Q_LORA, N_HEADS * LANES).astype(BF16)
    ukt = jnp.transpose(w_uk[0], (1, 2, 0))
    w_ukt_p = jnp.pad(ukt, ((0, 0), (QK_ROPE, LANES - QK_NOPE - QK_ROPE), (0, 0))).astype(BF16)
    w_uv_h = jnp.transpose(w_uv[0], (1, 0, 2)).astype(BF16)
    hw = N_HEADS * LANES
    w_uk_p = jnp.pad(w_uk[0], ((0, 0), (0, 0), (QK_ROPE, LANES - QK_NOPE - QK_ROPE))).reshape(KV_LORA, hw).astype(BF16)
    w_uv_p = jnp.pad(w_uv[0], ((0, 0), (0, 0), (0, LANES - V_HEAD))).reshape(KV_LORA, hw).astype(BF16)
    pool_w_b = pool_w[0].astype(BF16)
    w_router_p = jnp.pad(router_w[0], ((0, 0), (0, LANES - N_EXPERTS))).astype(F32)
    cos_t, sin_t = _rope_tables(past_len)

    x_r = x_prompt.reshape(N_R, D_MODEL)
    x_sm = jnp.concatenate([x_sample.reshape(N_S, D_MODEL), meta_tokens.astype(F32),
                            jnp.zeros((TM - N_META, D_MODEL), F32)], axis=0)

    upool, q_all, k_all, v_all, newkv, qcat_sm = _inproj(
        x_r, x_sm, vec(norm_mix_e[0]), w_in_p, vec(q_norm[0]), vec(kv_norm[0]), w_uq_p, w_uk_p, w_uv_p,
        w_ukt_p, cos_t, sin_t)
    pscale = vec(pool_scale[0])
    pool_r = _pool_seq(upool, upool, pool_w_b, pscale, n_seq=BATCH, seq_len=SEQ, first_block=0,
                       halo_block=META_ROW // POOL_HALO, pos0=N_META)
    pool_m = _pool_seq(upool, jnp.zeros((POOL_HALO, POOL_WIDTH), F32), pool_w_b, pscale, n_seq=1,
                       seq_len=N_META, first_block=META_ROW // N_META, halo_block=0, pos0=0)
    upool_s = upool[N_R:N_RS].reshape(DEC_BATCH, DEC_SEQ, POOL_WIDTH)
    pool_s = _pool_step(jnp.transpose(state_pool[0], (1, 0, 2)), jnp.transpose(upool_s, (1, 0, 2)),
                        pool_w_b, pscale)
    pool_s = jnp.transpose(pool_s, (1, 0, 2)).reshape(N_S, POOL_WIDTH)
    pool_sm = jnp.concatenate([pool_s, pool_m, jnp.zeros((TM - N_META, POOL_WIDTH), BF16)], axis=0)

    meta_pad = ((0, LANES - N_META), (0, 0))
    k_meta = jnp.pad(k_all[META_ROW:META_ROW + N_META], meta_pad)
    v_meta = jnp.pad(v_all[META_ROW:META_ROW + N_META], meta_pad)
    attn_r = _attn_prompt(q_all, k_all, v_all, k_meta, v_meta)
    attn_m = _attn_meta(q_all, k_meta, v_meta)
    q_s = qcat_sm[:, :N_S].reshape(N_HEADS, DEC_BATCH, DEC_SEQ, KCAT)
    q_s = jnp.transpose(q_s, (1, 0, 2, 3)).reshape(DEC_BATCH, DEC_ROWS, KCAT)
    k_new = jnp.pad(newkv[N_R:N_RS].reshape(DEC_BATCH, DEC_SEQ, KV_LORA + QK_ROPE),
                    ((0, 0), (0, DEC_NEW_PAD - DEC_SEQ), (0, KCAT - KV_LORA - QK_ROPE))).astype(BF16)
    o_lat = _attn_step(page_table.astype(jnp.int32), q_s, k_new, jnp.transpose(cache_mla, (0, 1, 3, 2)))
    o_lat = jnp.transpose(o_lat.reshape(DEC_BATCH, N_HEADS, DEC_SEQ, KV_LORA), (0, 2, 1, 3))
    attn_s = _uv_proj(o_lat.reshape(N_S, N_HEADS * KV_LORA), w_uv_h)
    attn_sm = jnp.concatenate([attn_s, attn_m, jnp.zeros((TM - N_META, ATTN_WIDTH), BF16)], axis=0)

    h2 = _mix_ffn(x_r, x_sm, pool_r, pool_sm, attn_r, attn_sm, w_out_e[0].astype(BF16),
                  vec(norm_ffn_e[0]), w_gu_e[0].astype(BF16), w_down_e[0].astype(BF16))

    u = _glu(h2, vec(norm_mix_o[0]), w_pw1[0].astype(BF16), vec(b_pw1[0]))
    conv_args = (conv_w[0].astype(F32), vec(conv_b[0]), vec(cnorm_g[0]), vec(cnorm_b[0]),
                 w_pw2[0].astype(BF16), vec(b_pw2[0]))
    halo0 = jnp.concatenate([jnp.zeros((CONV_HALO - N_META, CONV_CH), F32),
                             u[META_ROW:META_ROW + N_META]], axis=0)
    h3_r = _conv_seq(u, halo0, h2, *conv_args)
    u_s = u[N_R:N_RS].reshape(DEC_BATCH, DEC_SEQ, CONV_CH)
    h2_s = h2[N_R:N_RS].reshape(DEC_BATCH, DEC_SEQ, D_MODEL)
    h3_s = _conv_step(jnp.transpose(state_conv[0], (1, 0, 2)), jnp.transpose(u_s, (1, 0, 2)),
                      jnp.transpose(h2_s, (1, 0, 2)), *conv_args)
    h3_s = jnp.transpose(h3_s, (1, 0, 2)).reshape(N_S, D_MODEL)

    y_r, y_s = _moe(h3_r, h3_s, vec(norm_ffn_o[0]), w_router_p, w_gu_x[0].astype(BF16),
                    w_down_x[0].astype(BF16), vec(final_norm))
    y_prompt = y_r.reshape(BATCH, SEQ, D_MODEL)
    y_sample = y_s.reshape(DEC_BATCH, DEC_SEQ, D_MODEL)

    kv_w = KV_LORA + QK_ROPE
    kv_meta = jnp.broadcast_to(newkv[META_ROW:META_ROW + N_META][None], (BATCH, N_META, kv_w))
    new_mla_prompt = jnp.concatenate([kv_meta, newkv[:N_R].reshape(BATCH, SEQ, kv_w)], axis=1)[None]
    new_mla_sample = newkv[N_R:N_RS].reshape(1, DEC_BATCH, DEC_SEQ, kv_w)
    new_pool_prompt = upool[:N_R].reshape(BATCH, SEQ, POOL_WIDTH)[:, SEQ - POOL_BUF:][None]
    new_pool_sample = jnp.concatenate([state_pool[0][:, DEC_SEQ:], upool_s], axis=1)[None]
    new_conv_prompt = u[:N_R].reshape(BATCH, SEQ, CONV_CH)[:, SEQ - (CONV_W - 1):][None]
    new_conv_sample = jnp.concatenate([state_conv[0][:, DEC_SEQ:].astype(F32), u_s], axis=1)[None]
    return (y_prompt, y_sample, new_mla_prompt, new_mla_sample, new_pool_prompt, new_pool_sample,
            new_conv_prompt, new_conv_sample)
```

```python
import functools

import jax
import jax.numpy as jnp
from jax import lax
from jax.experimental import pallas as pl
from jax.experimental.pallas import tpu as pltpu

F32 = jnp.float32
BF16 = jnp.bfloat16

D_MODEL = 1024
BATCH = 16
SEQ = 2048
DEC_BATCH = 128
DEC_SEQ = 4
PAGE_SIZE = 128
N_META = 16
POOL_WINDOWS = (2, 4, 8, 16)
POOL_WIDTH = 512
POOL_GW = 128
POOL_BUF = 15
N_HEADS = 8
QK_NOPE = 64
QK_ROPE = 32
V_HEAD = 64
Q_LORA = 384
KV_LORA = 256
ROPE_THETA = 10000.0
ATTN_WIDTH = N_HEADS * V_HEAD
ATTN_SCALE = (QK_NOPE + QK_ROPE) ** -0.5
CONV_CH = D_MODEL
CONV_W = 31
D_FF = 2816
N_EXPERTS = 8
EXPERT_FF = 1408
EPS = 1e-6
NEG_INF = -1e30

LANES = 128
TM = 256
N_R = BATCH * SEQ
N_S = DEC_BATCH * DEC_SEQ
N_RS = N_R + N_S
R_TILES = N_R // TM
S_TILES = N_S // TM
RS_TILES = R_TILES + S_TILES
SM_ROWS = N_S + TM
NF = N_R + SM_ROWS
F_TILES = NF // TM
META_ROW = N_RS
SEQ_TILES = SEQ // TM
IN_W_PAD = 1280
KCAT = KV_LORA + LANES
QCAT = N_HEADS * KCAT
TQ = 256
PAGES_PER_STEP = 32
VMEM_LIMIT = 56 * 1024 * 1024


def _cparams(sem):
    return pltpu.CompilerParams(dimension_semantics=sem, vmem_limit_bytes=VMEM_LIMIT)


def _const_spec(shape):
    nd = len(shape)
    return pl.BlockSpec(shape, lambda *_: (0,) * nd)


def _rms(x, g):
    return x * lax.rsqrt(jnp.mean(x * x, axis=-1, keepdims=True) + EPS) * g


def _silu(x):
    return x * jax.nn.sigmoid(x)


def _dot(a, b):
    return jnp.dot(a, b, preferred_element_type=F32)


def _dot_nt(a, b):
    return lax.dot_general(a, b, (((1,), (1,)), ((), ())), preferred_element_type=F32)


def _r_idx(i):
    return jnp.minimum(i, R_TILES - 1)


def _sm_idx(i):
    return jnp.maximum(i - R_TILES, 0)


def _inproj_kernel(xr_ref, xsm_ref, gmix_ref, win_ref, qn_ref, kvn_ref, wuq_ref, wuk_ref, wuv_ref,
                   wukt_ref, cos_ref, sin_ref, upool_ref, q_ref, k_ref, v_ref, newkv_ref, qcat_ref):
    i = pl.program_id(0)
    x = jnp.where(i < R_TILES, xr_ref[...], xsm_ref[...])
    xn = _rms(x, gmix_ref[...]).astype(BF16)
    a = _dot(xn, win_ref[...])
    upool_ref[...] = a[:, :POOL_WIDTH]
    c0 = POOL_WIDTH
    c1 = c0 + Q_LORA
    c2 = c1 + KV_LORA
    cqn = _rms(a[:, c0:c1], qn_ref[...]).astype(BF16)
    q = _dot(cqn, wuq_ref[...])
    ckv = _rms(a[:, c1:c2], kvn_ref[...])
    cosv = cos_ref[...]
    sinv = sin_ref[...]
    lane = lax.broadcasted_iota(jnp.int32, (TM, LANES), 1)
    half = QK_ROPE // 2

    def rope(v):
        rolled = jnp.where(lane < half, pltpu.roll(v, LANES - half, 1), pltpu.roll(v, half, 1))
        return v * cosv + rolled * sinv

    kr = rope(a[:, c2:c2 + LANES])
    newkv_ref[:, :KV_LORA] = ckv
    newkv_ref[:, KV_LORA:] = kr[:, :QK_ROPE]
    ckv_b = ckv.astype(BF16)
    k_all = _dot(ckv_b, wuk_ref[...]) + jnp.concatenate([kr] * N_HEADS, axis=1)
    k_ref[...] = k_all.astype(BF16)
    v_ref[...] = _dot(ckv_b, wuv_ref[...]).astype(BF16)
    q_rot = [rope(q[:, h * LANES:(h + 1) * LANES]) for h in range(N_HEADS)]
    for h in range(N_HEADS):
        q_ref[:, h * LANES:(h + 1) * LANES] = q_rot[h].astype(BF16)

    @pl.when(i >= R_TILES)
    def _():
        for h in range(N_HEADS):
            qcat_ref[h, :, :KV_LORA] = _dot(q_rot[h].astype(BF16), wukt_ref[h]).astype(BF16)
            qcat_ref[h, :, KV_LORA:] = jnp.where(lane < QK_ROPE, q_rot[h], 0.0).astype(BF16)


def _inproj(x_r, x_sm, gmix, w_in_p, q_norm, kv_norm, w_uq_p, w_uk_p, w_uv_p, w_ukt_p, cos_t, sin_t):
    def tab_idx(i):
        return jnp.where(i < R_TILES, i % SEQ_TILES, i - R_TILES + SEQ_TILES)

    hw = N_HEADS * LANES
    row = lambda w: pl.BlockSpec((TM, w), lambda i: (i, 0))
    return pl.pallas_call(
        _inproj_kernel,
        grid=(F_TILES,),
        in_specs=[
            pl.BlockSpec((TM, D_MODEL), lambda i: (_r_idx(i), 0)),
            pl.BlockSpec((TM, D_MODEL), lambda i: (_sm_idx(i), 0)),
            _const_spec((1, D_MODEL)),
            _const_spec((D_MODEL, IN_W_PAD)),
            _const_spec((1, Q_LORA)),
            _const_spec((1, KV_LORA)),
            _const_spec((Q_LORA, hw)),
            _const_spec((KV_LORA, hw)),
            _const_spec((KV_LORA, hw)),
            _const_spec((N_HEADS, LANES, KV_LORA)),
            pl.BlockSpec((TM, LANES), lambda i: (tab_idx(i), 0)),
            pl.BlockSpec((TM, LANES), lambda i: (tab_idx(i), 0)),
        ],
        out_specs=[row(POOL_WIDTH), row(hw), row(hw), row(hw), row(KV_LORA + QK_ROPE),
                   pl.BlockSpec((N_HEADS, TM, KCAT), lambda i: (0, _sm_idx(i), 0))],
        out_shape=[
            jax.ShapeDtypeStruct((NF, POOL_WIDTH), F32),
            jax.ShapeDtypeStruct((NF, hw), BF16),
            jax.ShapeDtypeStruct((NF, hw), BF16),
            jax.ShapeDtypeStruct((NF, hw), BF16),
            jax.ShapeDtypeStruct((NF, KV_LORA + QK_ROPE), F32),
            jax.ShapeDtypeStruct((N_HEADS, SM_ROWS, KCAT), BF16),
        ],
        compiler_params=_cparams(("arbitrary",)),
        name="inproj",
    )(x_r, x_sm, gmix, w_in_p, q_norm, kv_norm, w_uq_p, w_uk_p, w_uv_p, w_ukt_p, cos_t, sin_t)


POOL_RC = 256
POOL_HALO = 16


def _pool_seq_kernel(u_ref, halo_ref, pw_ref, scale_ref, out_ref, z_ref, *, seq_len, pos0):
    z_ref[0:POOL_HALO, :] = halo_ref[...]
    z_ref[POOL_HALO:POOL_HALO + seq_len, :] = u_ref[...]
    rc = min(POOL_RC, seq_len)
    for g, w in enumerate(POOL_WINDOWS):
        sl = slice(g * POOL_GW, (g + 1) * POOL_GW)
        for r0 in range(0, seq_len, rc):
            base = POOL_HALO + r0
            cur = z_ref[base:base + rc, sl]
            acc = cur
            for k in range(1, w):
                acc = acc + z_ref[base - k:base - k + rc, sl]
            if pos0 + 1 >= w:
                mean = acc / float(w)
            else:
                pos = lax.broadcasted_iota(jnp.int32, (rc, POOL_GW), 0) + (pos0 + r0)
                mean = acc / jnp.minimum(pos + 1, w).astype(F32)
            d = (mean - cur).astype(BF16)
            o = _dot(d, pw_ref[g]) * scale_ref[:, sl]
            out_ref[r0:r0 + rc, sl] = o.astype(out_ref.dtype)


def _pool_seq(u_flat, halo, pool_w, pool_scale, *, n_seq, seq_len, first_block, halo_block, pos0):
    kern = functools.partial(_pool_seq_kernel, seq_len=seq_len, pos0=pos0)
    return pl.pallas_call(
        kern,
        grid=(n_seq,),
        in_specs=[
            pl.BlockSpec((seq_len, POOL_WIDTH), lambda b: (first_block + b, 0)),
            pl.BlockSpec((POOL_HALO, POOL_WIDTH), lambda b: (halo_block, 0)),
            _const_spec((len(POOL_WINDOWS), POOL_GW, POOL_GW)),
            _const_spec((1, POOL_WIDTH)),
        ],
        out_specs=pl.BlockSpec((seq_len, POOL_WIDTH), lambda b: (b, 0)),
        out_shape=jax.ShapeDtypeStruct((n_seq * seq_len, POOL_WIDTH), BF16),
        scratch_shapes=[pltpu.VMEM((POOL_HALO + seq_len, POOL_WIDTH), F32)],
        compiler_params=_cparams(("arbitrary",)),
        name="pool_seq",
    )(u_flat, halo, pool_w, pool_scale)


def _pool_step_kernel(st_ref, u_ref, pw_ref, scale_ref, out_ref):
    rows = [st_ref[j] for j in range(POOL_BUF)] + [u_ref[t] for t in range(DEC_SEQ)]
    for t in range(DEC_SEQ):
        cur = rows[POOL_BUF + t]
        for g, w in enumerate(POOL_WINDOWS):
            sl = slice(g * POOL_GW, (g + 1) * POOL_GW)
            acc = cur[:, sl]
            for k in range(1, w):
                acc = acc + rows[POOL_BUF + t - k][:, sl]
            d = (acc / float(w) - cur[:, sl]).astype(BF16)
            o = _dot(d, pw_ref[g]) * scale_ref[:, sl]
            out_ref[t, :, sl] = o.astype(out_ref.dtype)


def _pool_step(state_t, u_t, pool_w, pool_scale):
    return pl.pallas_call(
        _pool_step_kernel,
        out_shape=jax.ShapeDtypeStruct((DEC_SEQ, DEC_BATCH, POOL_WIDTH), BF16),
        compiler_params=pltpu.CompilerParams(vmem_limit_bytes=VMEM_LIMIT),
        name="pool_step",
    )(state_t, u_t, pool_w, pool_scale)


def _softmax_update(carry, s, v):
    m, l, acc = carry
    m_new = jnp.maximum(m, jnp.max(s, axis=1, keepdims=True))
    alpha = jnp.exp(m - m_new)
    p = jnp.exp(s - m_new)
    l = alpha * l + jnp.sum(p, axis=1, keepdims=True)
    acc = alpha * acc + _dot(p.astype(BF16), v)
    return m_new, l, acc


def _softmax_init(s, v):
    m = jnp.max(s, axis=1, keepdims=True)
    p = jnp.exp(s - m)
    return m, jnp.sum(p, axis=1, keepdims=True), _dot(p.astype(BF16), v)


def _attn_prompt_kernel(q_ref, k_ref, v_ref, km_ref, vm_ref, o_ref, mpart, mrep, lpart, acc, sbuf):
    qi = pl.program_id(1)
    lane = lax.broadcasted_iota(jnp.int32, (TQ, LANES), 1)
    meta_ok = lane < N_META
    causal = (lax.broadcasted_iota(jnp.int32, (TQ, TQ), 1)
              <= lax.broadcasted_iota(jnp.int32, (TQ, TQ), 0))
    hs = [slice(h * LANES, (h + 1) * LANES) for h in range(N_HEADS)]

    def keys(j):
        return pl.ds(pl.multiple_of(j * TQ, TQ), TQ)

    def raw_scores(h, kb, mask):
        s = _dot_nt(q_ref[:, hs[h]], kb)
        return s if mask is None else jnp.where(mask, s, NEG_INF)

    def max_sweep(j, mask):
        for h in range(N_HEADS):
            s = raw_scores(h, k_ref[keys(j), hs[h]], mask)
            sbuf[h, j] = s
            mpart[h] = jnp.maximum(mpart[h], jnp.maximum(s[:, :LANES], s[:, LANES:]))

    for h in range(N_HEADS):
        mpart[h] = raw_scores(h, km_ref[:, hs[h]], meta_ok)
    pl.loop(0, qi)(lambda j: max_sweep(j, None))
    max_sweep(qi, causal)
    for h in range(N_HEADS):
        m = jnp.max(mpart[h], axis=1, keepdims=True) * ATTN_SCALE
        mrep[h] = jnp.broadcast_to(m, (TQ, LANES))

    def probs(h, s):
        m = mrep[h]
        if s.shape[1] > LANES:
            m = jnp.concatenate([m] * (s.shape[1] // LANES), axis=1)
        return jnp.exp(s * ATTN_SCALE - m)

    for h in range(N_HEADS):
        p = probs(h, raw_scores(h, km_ref[:, hs[h]], meta_ok))
        lpart[h] = p
        acc[h] = _dot(p.astype(BF16), vm_ref[:, hs[h]])

    @pl.loop(0, qi + 1)
    def _(j):
        for h in range(N_HEADS):
            p = probs(h, sbuf[h, j])
            lpart[h] += p[:, :LANES] + p[:, LANES:]
            acc[h] += _dot(p.astype(BF16), v_ref[keys(j), hs[h]])

    outs = [(acc[h] / jnp.sum(lpart[h], axis=1, keepdims=True))[:, :V_HEAD] for h in range(N_HEADS)]
    o_ref[...] = jnp.concatenate(outs, axis=1).astype(o_ref.dtype)


def _attn_prompt(q_all, k_all, v_all, k_meta, v_meta):
    nq = SEQ // TQ
    hw = N_HEADS * LANES
    stat = pltpu.VMEM((N_HEADS, TQ, LANES), F32)
    return pl.pallas_call(
        _attn_prompt_kernel,
        grid=(BATCH, nq),
        in_specs=[
            pl.BlockSpec((TQ, hw), lambda b, i: (b * nq + i, 0)),
            pl.BlockSpec((SEQ, hw), lambda b, i: (b, 0)),
            pl.BlockSpec((SEQ, hw), lambda b, i: (b, 0)),
            _const_spec((LANES, hw)),
            _const_spec((LANES, hw)),
        ],
        out_specs=pl.BlockSpec((TQ, ATTN_WIDTH), lambda b, i: (b * nq + i, 0)),
        out_shape=jax.ShapeDtypeStruct((N_R, ATTN_WIDTH), BF16),
        scratch_shapes=[stat, stat, stat, stat, pltpu.VMEM((N_HEADS, nq, TQ, TQ), F32)],
        compiler_params=_cparams(("arbitrary", "arbitrary")),
        name="attn_prompt",
    )(q_all, k_all, v_all, k_meta, v_meta)


def _attn_meta_kernel(q_ref, km_ref, vm_ref, o_ref):
    causal = (lax.broadcasted_iota(jnp.int32, (N_META, LANES), 1)
              <= lax.broadcasted_iota(jnp.int32, (N_META, LANES), 0))
    outs = []
    for h in range(N_HEADS):
        sl = slice(h * LANES, (h + 1) * LANES)
        s = jnp.where(causal, _dot_nt(q_ref[:, sl], km_ref[:, sl]), NEG_INF) * ATTN_SCALE
        p = jnp.exp(s - jnp.max(s, axis=1, keepdims=True))
        o = _dot(p.astype(BF16), vm_ref[:, sl]) / jnp.sum(p, axis=1, keepdims=True)
        outs.append(o[:, :V_HEAD])
    o_ref[...] = jnp.concatenate(outs, axis=1).astype(o_ref.dtype)


def _attn_meta(q_all, k_meta, v_meta):
    hw = N_HEADS * LANES
    return pl.pallas_call(
        _attn_meta_kernel,
        grid=(1,),
        in_specs=[
            pl.BlockSpec((N_META, hw), lambda i: (META_ROW // N_META, 0)),
            _const_spec((LANES, hw)),
            _const_spec((LANES, hw)),
        ],
        out_specs=pl.BlockSpec((N_META, ATTN_WIDTH), lambda i: (0, 0)),
        out_shape=jax.ShapeDtypeStruct((N_META, ATTN_WIDTH), BF16),
        compiler_params=_cparams(("arbitrary",)),
        name="attn_meta",
    )(q_all, k_meta, v_meta)


DEC_ROWS = N_HEADS * DEC_SEQ
DEC_NEW_PAD = 16
CHUNK_KEYS = PAGES_PER_STEP * PAGE_SIZE
SUB_KEYS = CHUNK_KEYS // 2


def _attn_step_kernel(pt_ref, q_ref, knew_ref, cache_hbm, o_ref, pbuf, kbuf, m_ref, l_ref, acc_ref, sems):
    b = pl.program_id(0)
    c = pl.program_id(1)
    n_chunks = pl.num_programs(1)
    step = b * n_chunks + c
    slot = step % 2

    def page_copy(bb, cc, p, sl):
        page = pt_ref[bb, cc * PAGES_PER_STEP + p]
        return pltpu.make_async_copy(cache_hbm.at[0, page], pbuf.at[sl, p], sems.at[sl])

    @pl.when(step == 0)
    def _():
        for p in range(PAGES_PER_STEP):
            page_copy(0, 0, p, 0).start()

    last_c = c == n_chunks - 1
    nb = jnp.where(last_c, b + 1, b)
    nc = jnp.where(last_c, 0, c + 1)

    @pl.when(nb < pl.num_programs(0))
    def _():
        for p in range(PAGES_PER_STEP):
            page_copy(nb, nc, p, 1 - slot).start()

    q = q_ref[0]

    @pl.when(c == 0)
    def _():
        kn = knew_ref[0]
        row = lax.broadcasted_iota(jnp.int32, (DEC_ROWS, DEC_NEW_PAD), 0)
        col = lax.broadcasted_iota(jnp.int32, (DEC_ROWS, DEC_NEW_PAD), 1)
        s = jnp.where(col <= row % DEC_SEQ, _dot_nt(q, kn) * ATTN_SCALE, NEG_INF)
        m, l, acc = _softmax_init(s, kn[:, :KV_LORA])
        m_ref[...] = m
        l_ref[...] = l
        acc_ref[...] = acc

    for p in range(PAGES_PER_STEP):
        page_copy(b, c, p, slot).wait()
    kv_w = KV_LORA + QK_ROPE
    kbuf[kv_w:, :] = jnp.zeros((KCAT - kv_w, CHUNK_KEYS), BF16)
    for p in range(PAGES_PER_STEP):
        kbuf[:kv_w, p * PAGE_SIZE:(p + 1) * PAGE_SIZE] = pbuf[slot, p].astype(BF16)
    m, l, acc = m_ref[...], l_ref[...], acc_ref[...]
    for h in range(CHUNK_KEYS // SUB_KEYS):
        ks = kbuf[:, h * SUB_KEYS:(h + 1) * SUB_KEYS]
        s = _dot(q, ks) * ATTN_SCALE
        m_new = jnp.maximum(m, jnp.max(s, axis=1, keepdims=True))
        alpha = jnp.exp(m - m_new)
        p_un = jnp.exp(s - m_new)
        l = alpha * l + jnp.sum(p_un, axis=1, keepdims=True)
        acc = alpha * acc + _dot_nt(p_un.astype(BF16), ks[:KV_LORA, :])
        m = m_new
    m_ref[...] = m
    l_ref[...] = l
    acc_ref[...] = acc

    @pl.when(last_c)
    def _():
        o_ref[0] = (acc / l).astype(o_ref.dtype)


def _attn_step(page_table, q_s, k_new, cache_t):
    n_pages = page_table.shape[1]
    assert n_pages % PAGES_PER_STEP == 0
    n_chunks = n_pages // PAGES_PER_STEP
    kv_w = cache_t.shape[2]
    grid_spec = pltpu.PrefetchScalarGridSpec(
        num_scalar_prefetch=1,
        grid=(DEC_BATCH, n_chunks),
        in_specs=[
            pl.BlockSpec((1, DEC_ROWS, KCAT), lambda b, c, pt: (b, 0, 0)),
            pl.BlockSpec((1, DEC_NEW_PAD, KCAT), lambda b, c, pt: (b, 0, 0)),
            pl.BlockSpec(memory_space=pl.ANY),
        ],
        out_specs=pl.BlockSpec((1, DEC_ROWS, KV_LORA), lambda b, c, pt: (b, 0, 0)),
        scratch_shapes=[
            pltpu.VMEM((2, PAGES_PER_STEP, kv_w, PAGE_SIZE), F32),
            pltpu.VMEM((KCAT, CHUNK_KEYS), BF16),
            pltpu.VMEM((DEC_ROWS, 1), F32),
            pltpu.VMEM((DEC_ROWS, 1), F32),
            pltpu.VMEM((DEC_ROWS, KV_LORA), F32),
            pltpu.SemaphoreType.DMA((2,)),
        ],
    )
    return pl.pallas_call(
        _attn_step_kernel,
        grid_spec=grid_spec,
        out_shape=jax.ShapeDtypeStruct((DEC_BATCH, DEC_ROWS, KV_LORA), BF16),
        compiler_params=_cparams(("arbitrary", "arbitrary")),
        name="attn_step",
    )(page_table, q_s, k_new, cache_t)


def _uv_kernel(o_ref, wuv_ref, out_ref):
    outs = [_dot(o_ref[:, h * KV_LORA:(h + 1) * KV_LORA], wuv_ref[h]) for h in range(N_HEADS)]
    out_ref[...] = jnp.concatenate(outs, axis=1).astype(out_ref.dtype)


def _uv_proj(o_lat, w_uv_h):
    return pl.pallas_call(
        _uv_kernel,
        out_shape=jax.ShapeDtypeStruct((o_lat.shape[0], ATTN_WIDTH), BF16),
        compiler_params=pltpu.CompilerParams(vmem_limit_bytes=VMEM_LIMIT),
        name="uv_proj",
    )(o_lat, w_uv_h)


def _mix_ffn_kernel(xr_ref, xsm_ref, pr_ref, psm_ref, ar_ref, asm_ref, wout_ref, g_ref, wgu_ref,
                    wd_ref, h_ref):
    i = pl.program_id(0)
    is_r = i < R_TILES
    x = jnp.where(is_r, xr_ref[...], xsm_ref[...])
    mix = jnp.concatenate([jnp.where(is_r, pr_ref[...], psm_ref[...]),
                           jnp.where(is_r, ar_ref[...], asm_ref[...])], axis=1)
    h1 = x + _dot(mix, wout_ref[...])
    xn = _rms(h1, g_ref[...]).astype(BF16)
    gu = _dot(xn, wgu_ref[...])
    act = (_silu(gu[:, :D_FF]) * gu[:, D_FF:]).astype(BF16)
    h_ref[...] = h1 + _dot(act, wd_ref[...])


def _mix_ffn(x_r, x_sm, pool_r, pool_sm, attn_r, attn_sm, w_out, g, w_gu, w_down):
    r_spec = lambda w: pl.BlockSpec((TM, w), lambda i: (_r_idx(i), 0))
    sm_spec = lambda w: pl.BlockSpec((TM, w), lambda i: (_sm_idx(i), 0))
    return pl.pallas_call(
        _mix_ffn_kernel,
        grid=(F_TILES,),
        in_specs=[
            r_spec(D_MODEL), sm_spec(D_MODEL),
            r_spec(POOL_WIDTH), sm_spec(POOL_WIDTH),
            r_spec(ATTN_WIDTH), sm_spec(ATTN_WIDTH),
            _const_spec((D_MODEL, D_MODEL)),
            _const_spec((1, D_MODEL)),
            _const_spec((D_MODEL, 2 * D_FF)),
            _const_spec((D_FF, D_MODEL)),
        ],
        out_specs=pl.BlockSpec((TM, D_MODEL), lambda i: (i, 0)),
        out_shape=jax.ShapeDtypeStruct((NF, D_MODEL), F32),
        compiler_params=_cparams(("arbitrary",)),
        name="mix_ffn",
    )(x_r, x_sm, pool_r, pool_sm, attn_r, attn_sm, w_out, g, w_gu, w_down)


def _glu_kernel(h_ref, g_ref, w_ref, b_ref, u_ref):
    xn = _rms(h_ref[...], g_ref[...]).astype(BF16)
    a = _dot(xn, w_ref[...]) + b_ref[...]
    u_ref[...] = a[:, :CONV_CH] * jax.nn.sigmoid(a[:, CONV_CH:])


def _glu(h, g, w_pw1, b_pw1):
    return pl.pallas_call(
        _glu_kernel,
        grid=(F_TILES,),
        in_specs=[
            pl.BlockSpec((TM, D_MODEL), lambda i: (i, 0)),
            _const_spec((1, D_MODEL)),
            _const_spec((D_MODEL, 2 * CONV_CH)),
            _const_spec((1, 2 * CONV_CH)),
        ],
        out_specs=pl.BlockSpec((TM, CONV_CH), lambda i: (i, 0)),
        out_shape=jax.ShapeDtypeStruct((NF, CONV_CH), F32),
        compiler_params=_cparams(("arbitrary",)),
        name="glu",
    )(h, g, w_pw1, b_pw1)


CONV_HALO = 32


def _ln_swish_pw2(c, h, cg_ref, cb_ref, w2_ref, b2_ref):
    xc = c - jnp.mean(c, axis=-1, keepdims=True)
    var = jnp.mean(xc * xc, axis=-1, keepdims=True)
    n = xc * lax.rsqrt(var + EPS) * cg_ref[...] + cb_ref[...]
    return h + (_dot(_silu(n).astype(BF16), w2_ref[...]) + b2_ref[...])


def _conv_seq_kernel(u_ref, prev_ref, halo0_ref, h_ref, cw_ref, cbias_ref, cg_ref, cb_ref, w2_ref,
                     b2_ref, out_ref, z_ref, c_ref):
    i = pl.program_id(0)
    z_ref[0:CONV_HALO, :] = jnp.where(i % SEQ_TILES == 0, halo0_ref[...], prev_ref[...])
    z_ref[CONV_HALO:, :] = u_ref[...]
    off = CONV_HALO - (CONV_W - 1)
    sub = 8
    for cc in range(CONV_CH // LANES):
        sl = slice(cc * LANES, (cc + 1) * LANES)
        acc = None
        for ph in range(sub):
            rows = TM + (sub if ph else 0)
            part = None
            for a in range((CONV_W - 1 + off) // sub + 1):
                k = sub * a + ph - off
                if 0 <= k < CONV_W:
                    term = z_ref[sub * a:sub * a + rows, sl] * cw_ref[k:k + 1, sl]
                    part = term if part is None else part + term
            part = part[ph:ph + TM]
            acc = part if acc is None else acc + part
        c_ref[:, sl] = acc + cbias_ref[:, sl]
    out_ref[...] = _ln_swish_pw2(c_ref[...], h_ref[...], cg_ref, cb_ref, w2_ref, b2_ref)


def _conv_seq(u, halo0, h, conv_w, conv_b, cn_g, cn_b, w_pw2, b_pw2):
    per = TM // CONV_HALO
    vec = _const_spec((1, CONV_CH))
    return pl.pallas_call(
        _conv_seq_kernel,
        grid=(R_TILES,),
        in_specs=[
            pl.BlockSpec((TM, CONV_CH), lambda i: (i, 0)),
            pl.BlockSpec((CONV_HALO, CONV_CH), lambda i: (jnp.maximum(i * per - 1, 0), 0)),
            _const_spec((CONV_HALO, CONV_CH)),
            pl.BlockSpec((TM, D_MODEL), lambda i: (i, 0)),
            _const_spec((CONV_W, CONV_CH)),
            vec, vec, vec,
            _const_spec((CONV_CH, D_MODEL)),
            vec,
        ],
        out_specs=pl.BlockSpec((TM, D_MODEL), lambda i: (i, 0)),
        out_shape=jax.ShapeDtypeStruct((N_R, D_MODEL), F32),
        scratch_shapes=[pltpu.VMEM((CONV_HALO + TM, CONV_CH), F32), pltpu.VMEM((TM, CONV_CH), F32)],
        compiler_params=_cparams(("arbitrary",)),
        name="conv_seq",
    )(u, u, halo0, h, conv_w, conv_b, cn_g, cn_b, w_pw2, b_pw2)


def _conv_step_kernel(st_ref, u_ref, h_ref, cw_ref, cbias_ref, cg_ref, cb_ref, w2_ref, b2_ref, out_ref):
    n_st = CONV_W - 1
    for t in range(DEC_SEQ):
        acc = None
        for k in range(CONV_W):
            j = t + k
            zr = st_ref[j] if j < n_st else u_ref[j - n_st]
            term = zr * cw_ref[k:k + 1, :]
            acc = term if acc is None else acc + term
        out_ref[t] = _ln_swish_pw2(acc + cbias_ref[...], h_ref[t], cg_ref, cb_ref, w2_ref, b2_ref)


def _conv_step(state_t, u_t, h_t, conv_w, conv_b, cn_g, cn_b, w_pw2, b_pw2):
    return pl.pallas_call(
        _conv_step_kernel,
        out_shape=jax.ShapeDtypeStruct((DEC_SEQ, DEC_BATCH, D_MODEL), F32),
        compiler_params=pltpu.CompilerParams(vmem_limit_bytes=VMEM_LIMIT),
        name="conv_step",
    )(state_t, u_t, h_t, conv_w, conv_b, cn_g, cn_b, w_pw2, b_pw2)


TG = 512
NT_MAX = -(-(2 * N_RS + N_EXPERTS * (TG - 1)) // TG)
NS = NT_MAX * TG
REC_I1, REC_I2, REC_G1, REC_G2, REC_P1, REC_P2 = range(6)


def _lane_col(x, lane, k):
    return jnp.sum(jnp.where(lane == k, x, 0.0), axis=1, keepdims=True)


def _router_kernel(hr_ref, hs_ref, g_ref, wrh_ref, wrl_ref, rec_ref, cnt_ref, base_ref):
    i = pl.program_id(0)

    @pl.when(i == 0)
    def _():
        base_ref[...] = jnp.zeros_like(base_ref)

    h = jnp.where(i < R_TILES, hr_ref[...], hs_ref[...])
    xn = _rms(h, g_ref[...])
    xh = xn.astype(BF16)
    xl = (xn - xh.astype(F32)).astype(BF16)
    logits = _dot(xh, wrh_ref[...]) + (_dot(xl, wrh_ref[...]) + _dot(xh, wrl_ref[...]))
    lane = lax.broadcasted_iota(jnp.int32, (TM, LANES), 1)
    logits = jnp.where(lane < N_EXPERTS, logits, -jnp.inf)
    v1 = jnp.max(logits, axis=1, keepdims=True)
    i1 = jnp.min(jnp.where(logits == v1, lane, LANES), axis=1, keepdims=True)
    rest = jnp.where(lane == i1, -jnp.inf, logits)
    v2 = jnp.max(rest, axis=1, keepdims=True)
    i2 = jnp.min(jnp.where(rest == v2, lane, LANES), axis=1, keepdims=True)
    e2 = jnp.exp(v2 - v1)
    den = 1.0 + e2
    chosen = jnp.where((lane == i1) | (lane == i2), 1.0, 0.0)
    tri = jnp.where(lax.broadcasted_iota(jnp.int32, (TM, TM), 1) < lax.broadcasted_iota(jnp.int32, (TM, TM), 0),
                    1.0, 0.0).astype(BF16)
    rank = _dot(tri, chosen.astype(BF16)) + base_ref[...]
    base_ref[...] = base_ref[...] + jnp.sum(chosen, axis=0, keepdims=True)
    cnt_ref[...] = base_ref[...]
    rec = jnp.zeros((TM, LANES), F32)
    for k, val in ((REC_I1, i1.astype(F32)), (REC_I2, i2.astype(F32)), (REC_G1, 1.0 / den), (REC_G2, e2 / den),
                   (REC_P1, _lane_col(rank, lane, i1)), (REC_P2, _lane_col(rank, lane, i2))):
        rec = jnp.where(lane == k, val, rec)
    rec_ref[...] = rec


def _router(h_r, h_s, g, w_router_hi, w_router_lo):
    return pl.pallas_call(
        _router_kernel,
        grid=(RS_TILES,),
        in_specs=[
            pl.BlockSpec((TM, D_MODEL), lambda i: (_r_idx(i), 0)),
            pl.BlockSpec((TM, D_MODEL), lambda i: (_sm_idx(i), 0)),
            _const_spec((1, D_MODEL)),
            _const_spec((D_MODEL, LANES)),
            _const_spec((D_MODEL, LANES)),
        ],
        out_specs=[pl.BlockSpec((TM, LANES), lambda i: (i, 0)), _const_spec((1, LANES))],
        out_shape=[jax.ShapeDtypeStruct((N_RS, LANES), F32), jax.ShapeDtypeStruct((1, LANES), F32)],
        scratch_shapes=[pltpu.VMEM((1, LANES), F32)],
        compiler_params=_cparams(("arbitrary",)),
        name="router",
    )(h_r, h_s, g, w_router_hi, w_router_lo)


def _row_copy(src, s_row, dst, d_row, sem):
    return pltpu.make_async_copy(src.at[pl.ds(s_row, 1), :], dst.at[pl.ds(d_row, 1), :], sem)


def _wait_rows(src, dst, sem, n_bufs):
    for _ in range(n_bufs):
        pltpu.make_async_copy(src, dst, sem).wait()


def _dispatch_kernel(pad_ref, slots_ref, hr_ref, hs_ref, g_ref, xs_hbm, xbuf, zbuf, sem, zsem):
    i = pl.program_id(0)
    h = jnp.where(i < R_TILES, hr_ref[...], hs_ref[...])
    xbuf[...] = _rms(h, g_ref[...])

    def issue(t, carry):
        _row_copy(xbuf, t, xs_hbm, slots_ref[0, 0, 2 * t], sem).start()
        _row_copy(xbuf, t, xs_hbm, slots_ref[0, 0, 2 * t + 1], sem).start()
        return carry

    lax.fori_loop(0, TM, issue, 0, unroll=8)

    @pl.when(i == 0)
    def _():
        zbuf[...] = jnp.zeros_like(zbuf)
        for e in range(N_EXPERTS):
            def zero_row(r, carry, e=e):
                _row_copy(zbuf, 0, xs_hbm, pad_ref[0, e] + r, zsem).start()
                return carry
            lax.fori_loop(0, pad_ref[1, e], zero_row, 0)
        for e in range(N_EXPERTS):
            def wait_row(r, carry):
                _row_copy(zbuf, 0, xs_hbm, 0, zsem).wait()
                return carry
            lax.fori_loop(0, pad_ref[1, e], wait_row, 0)

        def tail_copy(j):
            return pltpu.make_async_copy(zbuf, xs_hbm.at[pl.ds(pl.multiple_of(j * TM, TM), TM), :], zsem)

        first_tail = pad_ref[2, 0] * (TG // TM)

        def zero_tail(j, carry):
            tail_copy(j).start()
            return carry

        def wait_tail(j, carry):
            tail_copy(j).wait()
            return carry

        lax.fori_loop(first_tail, NS // TM, zero_tail, 0)
        lax.fori_loop(first_tail, NS // TM, wait_tail, 0)

    _wait_rows(xbuf, xs_hbm.at[pl.ds(0, TM), :], sem, 2)


def _dispatch(pad_info, slots, h_r, h_s, g):
    grid_spec = pltpu.PrefetchScalarGridSpec(
        num_scalar_prefetch=1,
        grid=(RS_TILES,),
        in_specs=[
            pl.BlockSpec((1, 1, 2 * TM), lambda i, pad: (i, 0, 0), memory_space=pltpu.SMEM),
            pl.BlockSpec((TM, D_MODEL), lambda i, pad: (_r_idx(i), 0)),
            pl.BlockSpec((TM, D_MODEL), lambda i, pad: (_sm_idx(i), 0)),
            pl.BlockSpec((1, D_MODEL), lambda i, pad: (0, 0)),
        ],
        out_specs=pl.BlockSpec(memory_space=pl.ANY),
        scratch_shapes=[
            pltpu.VMEM((TM, D_MODEL), F32),
            pltpu.VMEM((TM, D_MODEL), F32),
            pltpu.SemaphoreType.DMA(()),
            pltpu.SemaphoreType.DMA(()),
        ],
    )
    return pl.pallas_call(
        _dispatch_kernel,
        grid_spec=grid_spec,
        out_shape=jax.ShapeDtypeStruct((NS, D_MODEL), F32),
        compiler_params=_cparams(("arbitrary",)),
        name="dispatch",
    )(pad_info, slots, h_r, h_s, g)


def _experts_kernel(te_ref, nu_ref, x_ref, wgu_ref, wd_ref, y_ref):
    j = pl.program_id(0)

    @pl.when(j < nu_ref[0])
    def _():
        gu = _dot(x_ref[...].astype(BF16), wgu_ref[0])
        act = (_silu(gu[:, :EXPERT_FF]) * gu[:, EXPERT_FF:]).astype(BF16)
        y_ref[...] = _dot(act, wd_ref[0])

    @pl.when(j >= nu_ref[0])
    def _():
        y_ref[...] = jnp.zeros_like(y_ref)


def _experts(tile_expert, n_used, xs, w_gu_x, w_down_x):
    grid_spec = pltpu.PrefetchScalarGridSpec(
        num_scalar_prefetch=2,
        grid=(NT_MAX,),
        in_specs=[
            pl.BlockSpec((TG, D_MODEL), lambda j, te, nu: (jnp.minimum(j, nu[0] - 1), 0)),
            pl.BlockSpec((1, D_MODEL, 2 * EXPERT_FF), lambda j, te, nu: (te[j], 0, 0)),
            pl.BlockSpec((1, EXPERT_FF, D_MODEL), lambda j, te, nu: (te[j], 0, 0)),
        ],
        out_specs=pl.BlockSpec((TG, D_MODEL), lambda j, te, nu: (j, 0)),
    )
    return pl.pallas_call(
        _experts_kernel,
        grid_spec=grid_spec,
        out_shape=jax.ShapeDtypeStruct((NS, D_MODEL), F32),
        compiler_params=_cparams(("arbitrary",)),
        name="experts",
    )(tile_expert, n_used, xs, w_gu_x, w_down_x)


def _combine_kernel(slots_ref, hr_ref, hs_ref, rec_ref, fg_ref, ys_hbm, outr_ref, outs_ref, r1buf, r2buf, sem):
    i = pl.program_id(0)

    def issue(t, carry):
        _row_copy(ys_hbm, slots_ref[0, 0, 2 * t], r1buf, t, sem).start()
        _row_copy(ys_hbm, slots_ref[0, 0, 2 * t + 1], r2buf, t, sem).start()
        return carry

    lax.fori_loop(0, TM, issue, 0, unroll=8)
    h = jnp.where(i < R_TILES, hr_ref[...], hs_ref[...])
    lane = lax.broadcasted_iota(jnp.int32, (TM, LANES), 1)
    rec = rec_ref[...]
    g1 = _lane_col(rec, lane, REC_G1)
    g2 = _lane_col(rec, lane, REC_G2)
    _wait_rows(ys_hbm.at[pl.ds(0, TM), :], r1buf, sem, 2)
    res = _rms(h + g1 * r1buf[...] + g2 * r2buf[...], fg_ref[...])

    @pl.when(i < R_TILES)
    def _():
        outr_ref[...] = res

    @pl.when(i >= R_TILES)
    def _():
        outs_ref[...] = res


def _combine(slots, h_r, h_s, rec, fg, ys):
    return pl.pallas_call(
        _combine_kernel,
        grid=(RS_TILES,),
        in_specs=[
            pl.BlockSpec((1, 1, 2 * TM), lambda i: (i, 0, 0), memory_space=pltpu.SMEM),
            pl.BlockSpec((TM, D_MODEL), lambda i: (_r_idx(i), 0)),
            pl.BlockSpec((TM, D_MODEL), lambda i: (_sm_idx(i), 0)),
            pl.BlockSpec((TM, LANES), lambda i: (i, 0)),
            _const_spec((1, D_MODEL)),
            pl.BlockSpec(memory_space=pl.ANY),
        ],
        out_specs=[
            pl.BlockSpec((TM, D_MODEL), lambda i: (_r_idx(i), 0)),
            pl.BlockSpec((TM, D_MODEL), lambda i: (_sm_idx(i), 0)),
        ],
        out_shape=[jax.ShapeDtypeStruct((N_R, D_MODEL), F32), jax.ShapeDtypeStruct((N_S, D_MODEL), F32)],
        scratch_shapes=[
            pltpu.VMEM((TM, D_MODEL), F32),
            pltpu.VMEM((TM, D_MODEL), F32),
            pltpu.SemaphoreType.DMA(()),
        ],
        compiler_params=_cparams(("arbitrary",)),
        name="combine",
    )(slots, h_r, h_s, rec, fg, ys)


def _moe(h_r, h_s, g, w_router_hi, w_router_lo, w_gu_x, w_down_x, fg):
    rec, counts = _router(h_r, h_s, g, w_router_hi, w_router_lo)
    cnt = counts[0, :N_EXPERTS].astype(jnp.int32)
    tiles = (cnt + (TG - 1)) // TG
    eidx = jnp.arange(N_EXPERTS, dtype=jnp.int32)
    tile_end = jnp.sum(jnp.where(eidx[None, :] <= eidx[:, None], tiles[None, :], 0), axis=1)
    slot_off = (tile_end - tiles) * TG
    ids = rec[:, REC_I1:REC_I2 + 1].astype(jnp.int32)
    pos = rec[:, REC_P1:REC_P2 + 1].astype(jnp.int32)
    tok_off = jnp.sum(jnp.where(ids[..., None] == eidx, slot_off, 0), axis=-1)
    slots = (tok_off + pos).reshape(RS_TILES, 1, 2 * TM)
    tile_idx = jnp.arange(NT_MAX, dtype=jnp.int32)
    tile_expert = jnp.minimum(jnp.sum((tile_idx[:, None] >= tile_end[None, :]).astype(jnp.int32), axis=1),
                              N_EXPERTS - 1)
    n_used = tile_end[-1:]
    pad_info = jnp.stack([slot_off + cnt, tiles * TG - cnt, jnp.broadcast_to(n_used, (N_EXPERTS,))])
    xs = _dispatch(pad_info, slots, h_r, h_s, g)
    ys = _experts(tile_expert, n_used, xs, w_gu_x, w_down_x)
    return _combine(slots, h_r, h_s, rec, fg, ys)


def _rope_tables(past_len):
    pos = jnp.concatenate([
        jnp.arange(SEQ, dtype=F32) + float(N_META),
        jnp.tile(jnp.arange(DEC_SEQ, dtype=F32) + float(past_len), DEC_BATCH),
        jnp.arange(N_META, dtype=F32),
        jnp.zeros((TM - N_META,), F32),
    ])
    inv = ROPE_THETA ** (-jnp.arange(0, QK_ROPE, 2, dtype=F32) / QK_ROPE)
    ang = pos[:, None] * inv[None, :]
    cos, sin = jnp.cos(ang), jnp.sin(ang)
    rest = (pos.shape[0], LANES - QK_ROPE)
    return (jnp.concatenate([cos, cos, jnp.ones(rest, F32)], axis=1),
            jnp.concatenate([-sin, sin, jnp.zeros(rest, F32)], axis=1))


def kernel(x_prompt, x_sample, cache_mla, page_table, state_pool, state_conv, meta_tokens, norm_mix_e, w_in_e, pool_w, pool_scale, q_norm, kv_norm, w_uq, w_uk, w_uv, w_out_e, norm_ffn_e, w_gu_e, w_down_e, norm_mix_o, w_pw1, b_pw1, conv_w, conv_b, cnorm_g, cnorm_b, w_pw2, b_pw2, norm_ffn_o, router_w, w_gu_x, w_down_x, final_norm):
    assert x_prompt.shape == (BATCH, SEQ, D_MODEL) and x_sample.shape == (DEC_BATCH, DEC_SEQ, D_MODEL)
    assert norm_mix_e.shape[0] == 1 and norm_mix_o.shape[0] == 1
    past_len = page_table.shape[1] * PAGE_SIZE
    vec = lambda v: v.reshape(1, -1).astype(F32)

    w_in_p = jnp.pad(w_in_e[0], ((0, 0), (0, IN_W_PAD - w_in_e.shape[2]))).astype(BF16)
    uq = w_uq[0].reshape(Q_LORA, N_HEADS, QK_NOPE + QK_ROPE)
    w_uq_p = jnp.concatenate(
        [uq[..., QK_NOPE:], uq[..., :QK_NOPE], jnp.zeros((Q_LORA, N_HEADS, LANES - QK_NOPE - QK_ROPE), F32)],
        axis=-1).reshape(Q_LORA, N_HEADS * LANES).astype(BF16)
    ukt = jnp.transpose(w_uk[0], (1, 2, 0))
    w_ukt_p = jnp.pad(ukt, ((0, 0), (QK_ROPE, LANES - QK_NOPE - QK_ROPE), (0, 0))).astype(BF16)
    w_uv_h = jnp.transpose(w_uv[0], (1, 0, 2)).astype(BF16)
    hw = N_HEADS * LANES
    w_uk_p = jnp.pad(w_uk[0], ((0, 0), (0, 0), (QK_ROPE, LANES - QK_NOPE - QK_ROPE))).reshape(KV_LORA, hw).astype(BF16)
    w_uv_p = jnp.pad(w_uv[0], ((0, 0), (0, 0), (0, LANES - V_HEAD))).reshape(KV_LORA, hw).astype(BF16)
    pool_w_b = pool_w[0].astype(BF16)
    w_router_p = jnp.pad(router_w[0], ((0, 0), (0, LANES - N_EXPERTS))).astype(F32)
    w_router_hi = w_router_p.astype(BF16)
    w_router_lo = (w_router_p - w_router_hi.astype(F32)).astype(BF16)
    cos_t, sin_t = _rope_tables(past_len)

    x_r = x_prompt.reshape(N_R, D_MODEL)
    x_sm = jnp.concatenate([x_sample.reshape(N_S, D_MODEL), meta_tokens.astype(F32),
                            jnp.zeros((TM - N_META, D_MODEL), F32)], axis=0)

    upool, q_all, k_all, v_all, newkv, qcat_sm = _inproj(
        x_r, x_sm, vec(norm_mix_e[0]), w_in_p, vec(q_norm[0]), vec(kv_norm[0]), w_uq_p, w_uk_p, w_uv_p,
        w_ukt_p, cos_t, sin_t)
    pscale = vec(pool_scale[0])
    pool_r = _pool_seq(upool, upool, pool_w_b, pscale, n_seq=BATCH, seq_len=SEQ, first_block=0,
                       halo_block=META_ROW // POOL_HALO, pos0=N_META)
    pool_m = _pool_seq(upool, jnp.zeros((POOL_HALO, POOL_WIDTH), F32), pool_w_b, pscale, n_seq=1,
                       seq_len=N_META, first_block=META_ROW // N_META, halo_block=0, pos0=0)
    upool_s = upool[N_R:N_RS].reshape(DEC_BATCH, DEC_SEQ, POOL_WIDTH)
    pool_s = _pool_step(jnp.transpose(state_pool[0], (1, 0, 2)), jnp.transpose(upool_s, (1, 0, 2)),
                        pool_w_b, pscale)
    pool_s = jnp.transpose(pool_s, (1, 0, 2)).reshape(N_S, POOL_WIDTH)
    pool_sm = jnp.concatenate([pool_s, pool_m, jnp.zeros((TM - N_META, POOL_WIDTH), BF16)], axis=0)

    meta_pad = ((0, LANES - N_META), (0, 0))
    k_meta = jnp.pad(k_all[META_ROW:META_ROW + N_META], meta_pad)
    v_meta = jnp.pad(v_all[META_ROW:META_ROW + N_META], meta_pad)
    attn_r = _attn_prompt(q_all, k_all, v_all, k_meta, v_meta)
    attn_m = _attn_meta(q_all, k_meta, v_meta)
    q_s = qcat_sm[:, :N_S].reshape(N_HEADS, DEC_BATCH, DEC_SEQ, KCAT)
    q_s = jnp.transpose(q_s, (1, 0, 2, 3)).reshape(DEC_BATCH, DEC_ROWS, KCAT)
    k_new = jnp.pad(newkv[N_R:N_RS].reshape(DEC_BATCH, DEC_SEQ, KV_LORA + QK_ROPE),
                    ((0, 0), (0, DEC_NEW_PAD - DEC_SEQ), (0, KCAT - KV_LORA - QK_ROPE))).astype(BF16)
    o_lat = _attn_step(page_table.astype(jnp.int32), q_s, k_new, jnp.transpose(cache_mla, (0, 1, 3, 2)))
    o_lat = jnp.transpose(o_lat.reshape(DEC_BATCH, N_HEADS, DEC_SEQ, KV_LORA), (0, 2, 1, 3))
    attn_s = _uv_proj(o_lat.reshape(N_S, N_HEADS * KV_LORA), w_uv_h)
    attn_sm = jnp.concatenate([attn_s, attn_m, jnp.zeros((TM - N_META, ATTN_WIDTH), BF16)], axis=0)

    h2 = _mix_ffn(x_r, x_sm, pool_r, pool_sm, attn_r, attn_sm, w_out_e[0].astype(BF16),
                  vec(norm_ffn_e[0]), w_gu_e[0].astype(BF16), w_down_e[0].astype(BF16))

    u = _glu(h2, vec(norm_mix_o[0]), w_pw1[0].astype(BF16), vec(b_pw1[0]))
    conv_args = (conv_w[0].astype(F32), vec(conv_b[0]), vec(cnorm_g[0]), vec(cnorm_b[0]),
                 w_pw2[0].astype(BF16), vec(b_pw2[0]))
    halo0 = jnp.concatenate([jnp.zeros((CONV_HALO - N_META, CONV_CH), F32),
                             u[META_ROW:META_ROW + N_META]], axis=0)
    h3_r = _conv_seq(u, halo0, h2, *conv_args)
    u_s = u[N_R:N_RS].reshape(DEC_BATCH, DEC_SEQ, CONV_CH)
    h2_s = h2[N_R:N_RS].reshape(DEC_BATCH, DEC_SEQ, D_MODEL)
    h3_s = _conv_step(jnp.transpose(state_conv[0], (1, 0, 2)), jnp.transpose(u_s, (1, 0, 2)),
                      jnp.transpose(h2_s, (1, 0, 2)), *conv_args)
    h3_s = jnp.transpose(h3_s, (1, 0, 2)).reshape(N_S, D_MODEL)

    y_r, y_s = _moe(h3_r, h3_s, vec(norm_ffn_o[0]), w_router_hi, w_router_lo, w_gu_x[0].astype(BF16),
                    w_down_x[0].astype(BF16), vec(final_norm))
    y_prompt = y_r.reshape(BATCH, SEQ, D_MODEL)
    y_sample = y_s.reshape(DEC_BATCH, DEC_SEQ, D_MODEL)

    kv_w = KV_LORA + QK_ROPE
    kv_meta = jnp.broadcast_to(newkv[META_ROW:META_ROW + N_META][None], (BATCH, N_META, kv_w))
    new_mla_prompt = jnp.concatenate([kv_meta, newkv[:N_R].reshape(BATCH, SEQ, kv_w)], axis=1)[None]
    new_mla_sample = newkv[N_R:N_RS].reshape(1, DEC_BATCH, DEC_SEQ, kv_w)
    new_pool_prompt = jnp.stack([upool[(b + 1) * SEQ - POOL_BUF:(b + 1) * SEQ] for b in range(BATCH)])[None]
    new_pool_sample = jnp.concatenate([state_pool[0][:, DEC_SEQ:], upool_s], axis=1)[None]
    new_conv_prompt = jnp.stack([u[(b + 1) * SEQ - (CONV_W - 1):(b + 1) * SEQ] for b in range(BATCH)])[None]
    new_conv_sample = jnp.concatenate([state_conv[0][:, DEC_SEQ:].astype(F32), u_s], axis=1)[None]
    return (y_prompt, y_sample, new_mla_prompt, new_mla_sample, new_pool_prompt, new_pool_sample,
            new_conv_prompt, new_conv_sample)
```

```python
import functools

import jax
import jax.numpy as jnp
from jax import lax
from jax.experimental import pallas as pl
from jax.experimental.pallas import tpu as pltpu

F32 = jnp.float32
BF16 = jnp.bfloat16

D_MODEL = 1024
BATCH = 16
SEQ = 2048
DEC_BATCH = 128
DEC_SEQ = 4
PAGE_SIZE = 128
N_META = 16
POOL_WINDOWS = (2, 4, 8, 16)
POOL_WIDTH = 512
POOL_GW = 128
POOL_BUF = 15
N_HEADS = 8
QK_NOPE = 64
QK_ROPE = 32
V_HEAD = 64
Q_LORA = 384
KV_LORA = 256
ROPE_THETA = 10000.0
ATTN_WIDTH = N_HEADS * V_HEAD
ATTN_SCALE = (QK_NOPE + QK_ROPE) ** -0.5
CONV_CH = D_MODEL
CONV_W = 31
D_FF = 2816
N_EXPERTS = 8
EXPERT_FF = 1408
EPS = 1e-6
NEG_INF = -1e30

LANES = 128
TM = 256
N_R = BATCH * SEQ
N_S = DEC_BATCH * DEC_SEQ
N_RS = N_R + N_S
R_TILES = N_R // TM
S_TILES = N_S // TM
RS_TILES = R_TILES + S_TILES
SM_ROWS = N_S + TM
NF = N_R + SM_ROWS
F_TILES = NF // TM
META_ROW = N_RS
SEQ_TILES = SEQ // TM
IN_W_PAD = 1280
KCAT = KV_LORA + LANES
QCAT = N_HEADS * KCAT
TQ = 256
PAGES_PER_STEP = 32
VMEM_LIMIT = 56 * 1024 * 1024


def _cparams(sem):
    return pltpu.CompilerParams(dimension_semantics=sem, vmem_limit_bytes=VMEM_LIMIT)


def _const_spec(shape):
    nd = len(shape)
    return pl.BlockSpec(shape, lambda *_: (0,) * nd)


def _rms(x, g):
    return x * lax.rsqrt(jnp.mean(x * x, axis=-1, keepdims=True) + EPS) * g


def _silu(x):
    return x * jax.nn.sigmoid(x)


def _dot(a, b):
    return jnp.dot(a, b, preferred_element_type=F32)


def _dot_nt(a, b):
    return lax.dot_general(a, b, (((1,), (1,)), ((), ())), preferred_element_type=F32)


def _r_idx(i):
    return jnp.minimum(i, R_TILES - 1)


def _sm_idx(i):
    return jnp.maximum(i - R_TILES, 0)


def _inproj_kernel(xr_ref, xsm_ref, gmix_ref, win_ref, qn_ref, kvn_ref, wuq_ref, wuk_ref, wuv_ref,
                   wukt_ref, cos_ref, sin_ref, upool_ref, q_ref, k_ref, v_ref, newkv_ref, qcat_ref):
    i = pl.program_id(0)
    x = jnp.where(i < R_TILES, xr_ref[...], xsm_ref[...])
    xn = _rms(x, gmix_ref[...]).astype(BF16)
    a = _dot(xn, win_ref[...])
    upool_ref[...] = a[:, :POOL_WIDTH]
    c0 = POOL_WIDTH
    c1 = c0 + Q_LORA
    c2 = c1 + KV_LORA
    cqn = _rms(a[:, c0:c1], qn_ref[...]).astype(BF16)
    q = _dot(cqn, wuq_ref[...])
    ckv = _rms(a[:, c1:c2], kvn_ref[...])
    cosv = cos_ref[...]
    sinv = sin_ref[...]
    lane = lax.broadcasted_iota(jnp.int32, (TM, LANES), 1)
    half = QK_ROPE // 2

    def rope(v):
        rolled = jnp.where(lane < half, pltpu.roll(v, LANES - half, 1), pltpu.roll(v, half, 1))
        return v * cosv + rolled * sinv

    kr = rope(a[:, c2:c2 + LANES])
    newkv_ref[:, :KV_LORA] = ckv
    newkv_ref[:, KV_LORA:] = kr[:, :QK_ROPE]
    ckv_b = ckv.astype(BF16)
    k_all = _dot(ckv_b, wuk_ref[...]) + jnp.concatenate([kr] * N_HEADS, axis=1)
    k_ref[...] = k_all.astype(BF16)
    v_ref[...] = _dot(ckv_b, wuv_ref[...]).astype(BF16)
    q_rot = [rope(q[:, h * LANES:(h + 1) * LANES]) for h in range(N_HEADS)]
    for h in range(N_HEADS):
        q_ref[:, h * LANES:(h + 1) * LANES] = q_rot[h].astype(BF16)

    @pl.when(i >= R_TILES)
    def _():
        for h in range(N_HEADS):
            qcat_ref[h, :, :KV_LORA] = _dot(q_rot[h].astype(BF16), wukt_ref[h]).astype(BF16)
            qcat_ref[h, :, KV_LORA:] = jnp.where(lane < QK_ROPE, q_rot[h], 0.0).astype(BF16)


def _inproj(x_r, x_sm, gmix, w_in_p, q_norm, kv_norm, w_uq_p, w_uk_p, w_uv_p, w_ukt_p, cos_t, sin_t):
    def tab_idx(i):
        return jnp.where(i < R_TILES, i % SEQ_TILES, i - R_TILES + SEQ_TILES)

    hw = N_HEADS * LANES
    row = lambda w: pl.BlockSpec((TM, w), lambda i: (i, 0))
    return pl.pallas_call(
        _inproj_kernel,
        grid=(F_TILES,),
        in_specs=[
            pl.BlockSpec((TM, D_MODEL), lambda i: (_r_idx(i), 0)),
            pl.BlockSpec((TM, D_MODEL), lambda i: (_sm_idx(i), 0)),
            _const_spec((1, D_MODEL)),
            _const_spec((D_MODEL, IN_W_PAD)),
            _const_spec((1, Q_LORA)),
            _const_spec((1, KV_LORA)),
            _const_spec((Q_LORA, hw)),
            _const_spec((KV_LORA, hw)),
            _const_spec((KV_LORA, hw)),
            _const_spec((N_HEADS, LANES, KV_LORA)),
            pl.BlockSpec((TM, LANES), lambda i: (tab_idx(i), 0)),
            pl.BlockSpec((TM, LANES), lambda i: (tab_idx(i), 0)),
        ],
        out_specs=[row(POOL_WIDTH), row(hw), row(hw), row(hw), row(KV_LORA + QK_ROPE),
                   pl.BlockSpec((N_HEADS, TM, KCAT), lambda i: (0, _sm_idx(i), 0))],
        out_shape=[
            jax.ShapeDtypeStruct((NF, POOL_WIDTH), F32),
            jax.ShapeDtypeStruct((NF, hw), BF16),
            jax.ShapeDtypeStruct((NF, hw), BF16),
            jax.ShapeDtypeStruct((NF, hw), BF16),
            jax.ShapeDtypeStruct((NF, KV_LORA + QK_ROPE), F32),
            jax.ShapeDtypeStruct((N_HEADS, SM_ROWS, KCAT), BF16),
        ],
        compiler_params=_cparams(("arbitrary",)),
        name="inproj",
    )(x_r, x_sm, gmix, w_in_p, q_norm, kv_norm, w_uq_p, w_uk_p, w_uv_p, w_ukt_p, cos_t, sin_t)


POOL_RC = 256
POOL_HALO = 16


def _pool_seq_kernel(u_ref, halo_ref, pw_ref, scale_ref, out_ref, z_ref, *, seq_len, pos0):
    z_ref[0:POOL_HALO, :] = halo_ref[...]
    z_ref[POOL_HALO:POOL_HALO + seq_len, :] = u_ref[...]
    rc = min(POOL_RC, seq_len)
    for g, w in enumerate(POOL_WINDOWS):
        sl = slice(g * POOL_GW, (g + 1) * POOL_GW)
        for r0 in range(0, seq_len, rc):
            base = POOL_HALO + r0
            cur = z_ref[base:base + rc, sl]
            acc = cur
            for k in range(1, w):
                acc = acc + z_ref[base - k:base - k + rc, sl]
            if pos0 + 1 >= w:
                mean = acc / float(w)
            else:
                pos = lax.broadcasted_iota(jnp.int32, (rc, POOL_GW), 0) + (pos0 + r0)
                mean = acc / jnp.minimum(pos + 1, w).astype(F32)
            d = (mean - cur).astype(BF16)
            o = _dot(d, pw_ref[g]) * scale_ref[:, sl]
            out_ref[r0:r0 + rc, sl] = o.astype(out_ref.dtype)


def _pool_seq(u_flat, halo, pool_w, pool_scale, *, n_seq, seq_len, first_block, halo_block, pos0):
    kern = functools.partial(_pool_seq_kernel, seq_len=seq_len, pos0=pos0)
    return pl.pallas_call(
        kern,
        grid=(n_seq,),
        in_specs=[
            pl.BlockSpec((seq_len, POOL_WIDTH), lambda b: (first_block + b, 0)),
            pl.BlockSpec((POOL_HALO, POOL_WIDTH), lambda b: (halo_block, 0)),
            _const_spec((len(POOL_WINDOWS), POOL_GW, POOL_GW)),
            _const_spec((1, POOL_WIDTH)),
        ],
        out_specs=pl.BlockSpec((seq_len, POOL_WIDTH), lambda b: (b, 0)),
        out_shape=jax.ShapeDtypeStruct((n_seq * seq_len, POOL_WIDTH), BF16),
        scratch_shapes=[pltpu.VMEM((POOL_HALO + seq_len, POOL_WIDTH), F32)],
        compiler_params=_cparams(("arbitrary",)),
        name="pool_seq",
    )(u_flat, halo, pool_w, pool_scale)


def _pool_step_kernel(st_ref, u_ref, pw_ref, scale_ref, out_ref):
    rows = [st_ref[j] for j in range(POOL_BUF)] + [u_ref[t] for t in range(DEC_SEQ)]
    for t in range(DEC_SEQ):
        cur = rows[POOL_BUF + t]
        for g, w in enumerate(POOL_WINDOWS):
            sl = slice(g * POOL_GW, (g + 1) * POOL_GW)
            acc = cur[:, sl]
            for k in range(1, w):
                acc = acc + rows[POOL_BUF + t - k][:, sl]
            d = (acc / float(w) - cur[:, sl]).astype(BF16)
            o = _dot(d, pw_ref[g]) * scale_ref[:, sl]
            out_ref[t, :, sl] = o.astype(out_ref.dtype)


def _pool_step(state_t, u_t, pool_w, pool_scale):
    return pl.pallas_call(
        _pool_step_kernel,
        out_shape=jax.ShapeDtypeStruct((DEC_SEQ, DEC_BATCH, POOL_WIDTH), BF16),
        compiler_params=pltpu.CompilerParams(vmem_limit_bytes=VMEM_LIMIT),
        name="pool_step",
    )(state_t, u_t, pool_w, pool_scale)


def _softmax_update(carry, s, v):
    m, l, acc = carry
    m_new = jnp.maximum(m, jnp.max(s, axis=1, keepdims=True))
    alpha = jnp.exp(m - m_new)
    p = jnp.exp(s - m_new)
    l = alpha * l + jnp.sum(p, axis=1, keepdims=True)
    acc = alpha * acc + _dot(p.astype(BF16), v)
    return m_new, l, acc


def _softmax_init(s, v):
    m = jnp.max(s, axis=1, keepdims=True)
    p = jnp.exp(s - m)
    return m, jnp.sum(p, axis=1, keepdims=True), _dot(p.astype(BF16), v)


def _attn_prompt_kernel(q_ref, k_ref, v_ref, km_ref, vm_ref, o_ref, mpart, mrep, lpart, acc, sbuf):
    qi = pl.program_id(1)
    lane = lax.broadcasted_iota(jnp.int32, (TQ, LANES), 1)
    meta_ok = lane < N_META
    causal = (lax.broadcasted_iota(jnp.int32, (TQ, TQ), 1)
              <= lax.broadcasted_iota(jnp.int32, (TQ, TQ), 0))
    hs = [slice(h * LANES, (h + 1) * LANES) for h in range(N_HEADS)]

    def keys(j):
        return pl.ds(pl.multiple_of(j * TQ, TQ), TQ)

    def raw_scores(h, kb, mask):
        s = _dot_nt(q_ref[:, hs[h]], kb)
        return s if mask is None else jnp.where(mask, s, NEG_INF)

    def max_sweep(j, mask):
        for h in range(N_HEADS):
            s = raw_scores(h, k_ref[keys(j), hs[h]], mask)
            sbuf[h, j] = s
            mpart[h] = jnp.maximum(mpart[h], jnp.maximum(s[:, :LANES], s[:, LANES:]))

    for h in range(N_HEADS):
        mpart[h] = raw_scores(h, km_ref[:, hs[h]], meta_ok)
    pl.loop(0, qi)(lambda j: max_sweep(j, None))
    max_sweep(qi, causal)
    for h in range(N_HEADS):
        m = jnp.max(mpart[h], axis=1, keepdims=True) * ATTN_SCALE
        mrep[h] = jnp.broadcast_to(m, (TQ, LANES))

    def probs(h, s):
        m = mrep[h]
        if s.shape[1] > LANES:
            m = jnp.concatenate([m] * (s.shape[1] // LANES), axis=1)
        return jnp.exp(s * ATTN_SCALE - m)

    for h in range(N_HEADS):
        p = probs(h, raw_scores(h, km_ref[:, hs[h]], meta_ok))
        lpart[h] = p
        acc[h] = _dot(p.astype(BF16), vm_ref[:, hs[h]])

    @pl.loop(0, qi + 1)
    def _(j):
        for h in range(N_HEADS):
            p = probs(h, sbuf[h, j])
            lpart[h] += p[:, :LANES] + p[:, LANES:]
            acc[h] += _dot(p.astype(BF16), v_ref[keys(j), hs[h]])

    outs = [(acc[h] / jnp.sum(lpart[h], axis=1, keepdims=True))[:, :V_HEAD] for h in range(N_HEADS)]
    o_ref[...] = jnp.concatenate(outs, axis=1).astype(o_ref.dtype)


def _attn_prompt(q_all, k_all, v_all, k_meta, v_meta):
    nq = SEQ // TQ
    hw = N_HEADS * LANES
    stat = pltpu.VMEM((N_HEADS, TQ, LANES), F32)
    return pl.pallas_call(
        _attn_prompt_kernel,
        grid=(BATCH, nq),
        in_specs=[
            pl.BlockSpec((TQ, hw), lambda b, i: (b * nq + i, 0)),
            pl.BlockSpec((SEQ, hw), lambda b, i: (b, 0)),
            pl.BlockSpec((SEQ, hw), lambda b, i: (b, 0)),
            _const_spec((LANES, hw)),
            _const_spec((LANES, hw)),
        ],
        out_specs=pl.BlockSpec((TQ, ATTN_WIDTH), lambda b, i: (b * nq + i, 0)),
        out_shape=jax.ShapeDtypeStruct((N_R, ATTN_WIDTH), BF16),
        scratch_shapes=[stat, stat, stat, stat, pltpu.VMEM((N_HEADS, nq, TQ, TQ), F32)],
        compiler_params=_cparams(("arbitrary", "arbitrary")),
        name="attn_prompt",
    )(q_all, k_all, v_all, k_meta, v_meta)


def _attn_meta_kernel(q_ref, km_ref, vm_ref, o_ref):
    causal = (lax.broadcasted_iota(jnp.int32, (N_META, LANES), 1)
              <= lax.broadcasted_iota(jnp.int32, (N_META, LANES), 0))
    outs = []
    for h in range(N_HEADS):
        sl = slice(h * LANES, (h + 1) * LANES)
        s = jnp.where(causal, _dot_nt(q_ref[:, sl], km_ref[:, sl]), NEG_INF) * ATTN_SCALE
        p = jnp.exp(s - jnp.max(s, axis=1, keepdims=True))
        o = _dot(p.astype(BF16), vm_ref[:, sl]) / jnp.sum(p, axis=1, keepdims=True)
        outs.append(o[:, :V_HEAD])
    o_ref[...] = jnp.concatenate(outs, axis=1).astype(o_ref.dtype)


def _attn_meta(q_all, k_meta, v_meta):
    hw = N_HEADS * LANES
    return pl.pallas_call(
        _attn_meta_kernel,
        grid=(1,),
        in_specs=[
            pl.BlockSpec((N_META, hw), lambda i: (META_ROW // N_META, 0)),
            _const_spec((LANES, hw)),
            _const_spec((LANES, hw)),
        ],
        out_specs=pl.BlockSpec((N_META, ATTN_WIDTH), lambda i: (0, 0)),
        out_shape=jax.ShapeDtypeStruct((N_META, ATTN_WIDTH), BF16),
        compiler_params=_cparams(("arbitrary",)),
        name="attn_meta",
    )(q_all, k_meta, v_meta)


DEC_ROWS = N_HEADS * DEC_SEQ
DEC_NEW_PAD = 16
CHUNK_KEYS = PAGES_PER_STEP * PAGE_SIZE
SUB_KEYS = CHUNK_KEYS // 2


def _attn_step_kernel(pt_ref, q_ref, knew_ref, cache_hbm, o_ref, pbuf, kbuf, m_ref, l_ref, acc_ref, sems):
    b = pl.program_id(0)
    c = pl.program_id(1)
    n_chunks = pl.num_programs(1)
    step = b * n_chunks + c
    slot = step % 2

    def page_copy(bb, cc, p, sl):
        page = pt_ref[bb, cc * PAGES_PER_STEP + p]
        return pltpu.make_async_copy(cache_hbm.at[0, page], pbuf.at[sl, p], sems.at[sl])

    @pl.when(step == 0)
    def _():
        for p in range(PAGES_PER_STEP):
            page_copy(0, 0, p, 0).start()

    last_c = c == n_chunks - 1
    nb = jnp.where(last_c, b + 1, b)
    nc = jnp.where(last_c, 0, c + 1)

    @pl.when(nb < pl.num_programs(0))
    def _():
        for p in range(PAGES_PER_STEP):
            page_copy(nb, nc, p, 1 - slot).start()

    q = q_ref[0]

    @pl.when(c == 0)
    def _():
        kn = knew_ref[0]
        row = lax.broadcasted_iota(jnp.int32, (DEC_ROWS, DEC_NEW_PAD), 0)
        col = lax.broadcasted_iota(jnp.int32, (DEC_ROWS, DEC_NEW_PAD), 1)
        s = jnp.where(col <= row % DEC_SEQ, _dot_nt(q, kn) * ATTN_SCALE, NEG_INF)
        m, l, acc = _softmax_init(s, kn[:, :KV_LORA])
        m_ref[...] = m
        l_ref[...] = l
        acc_ref[...] = acc

    for p in range(PAGES_PER_STEP):
        page_copy(b, c, p, slot).wait()
    kv_w = KV_LORA + QK_ROPE
    kbuf[kv_w:, :] = jnp.zeros((KCAT - kv_w, CHUNK_KEYS), BF16)
    for p in range(PAGES_PER_STEP):
        kbuf[:kv_w, p * PAGE_SIZE:(p + 1) * PAGE_SIZE] = pbuf[slot, p].astype(BF16)
    m, l, acc = m_ref[...], l_ref[...], acc_ref[...]
    for h in range(CHUNK_KEYS // SUB_KEYS):
        ks = kbuf[:, h * SUB_KEYS:(h + 1) * SUB_KEYS]
        s = _dot(q, ks) * ATTN_SCALE
        m_new = jnp.maximum(m, jnp.max(s, axis=1, keepdims=True))
        alpha = jnp.exp(m - m_new)
        p_un = jnp.exp(s - m_new)
        l = alpha * l + jnp.sum(p_un, axis=1, keepdims=True)
        acc = alpha * acc + _dot_nt(p_un.astype(BF16), ks[:KV_LORA, :])
        m = m_new
    m_ref[...] = m
    l_ref[...] = l
    acc_ref[...] = acc

    @pl.when(last_c)
    def _():
        o_ref[0] = (acc / l).astype(o_ref.dtype)


def _attn_step(page_table, q_s, k_new, cache_t):
    n_pages = page_table.shape[1]
    assert n_pages % PAGES_PER_STEP == 0
    n_chunks = n_pages // PAGES_PER_STEP
    kv_w = cache_t.shape[2]
    grid_spec = pltpu.PrefetchScalarGridSpec(
        num_scalar_prefetch=1,
        grid=(DEC_BATCH, n_chunks),
        in_specs=[
            pl.BlockSpec((1, DEC_ROWS, KCAT), lambda b, c, pt: (b, 0, 0)),
            pl.BlockSpec((1, DEC_NEW_PAD, KCAT), lambda b, c, pt: (b, 0, 0)),
            pl.BlockSpec(memory_space=pl.ANY),
        ],
        out_specs=pl.BlockSpec((1, DEC_ROWS, KV_LORA), lambda b, c, pt: (b, 0, 0)),
        scratch_shapes=[
            pltpu.VMEM((2, PAGES_PER_STEP, kv_w, PAGE_SIZE), F32),
            pltpu.VMEM((KCAT, CHUNK_KEYS), BF16),
            pltpu.VMEM((DEC_ROWS, 1), F32),
            pltpu.VMEM((DEC_ROWS, 1), F32),
            pltpu.VMEM((DEC_ROWS, KV_LORA), F32),
            pltpu.SemaphoreType.DMA((2,)),
        ],
    )
    return pl.pallas_call(
        _attn_step_kernel,
        grid_spec=grid_spec,
        out_shape=jax.ShapeDtypeStruct((DEC_BATCH, DEC_ROWS, KV_LORA), BF16),
        compiler_params=_cparams(("arbitrary", "arbitrary")),
        name="attn_step",
    )(page_table, q_s, k_new, cache_t)


def _uv_kernel(o_ref, wuv_ref, out_ref):
    outs = [_dot(o_ref[:, h * KV_LORA:(h + 1) * KV_LORA], wuv_ref[h]) for h in range(N_HEADS)]
    out_ref[...] = jnp.concatenate(outs, axis=1).astype(out_ref.dtype)


def _uv_proj(o_lat, w_uv_h):
    return pl.pallas_call(
        _uv_kernel,
        out_shape=jax.ShapeDtypeStruct((o_lat.shape[0], ATTN_WIDTH), BF16),
        compiler_params=pltpu.CompilerParams(vmem_limit_bytes=VMEM_LIMIT),
        name="uv_proj",
    )(o_lat, w_uv_h)


def _mix_ffn_kernel(xr_ref, xsm_ref, pr_ref, psm_ref, ar_ref, asm_ref, wout_ref, g_ref, wgu_ref,
                    wd_ref, h_ref):
    i = pl.program_id(0)
    is_r = i < R_TILES
    x = jnp.where(is_r, xr_ref[...], xsm_ref[...])
    mix = jnp.concatenate([jnp.where(is_r, pr_ref[...], psm_ref[...]),
                           jnp.where(is_r, ar_ref[...], asm_ref[...])], axis=1)
    h1 = x + _dot(mix, wout_ref[...])
    xn = _rms(h1, g_ref[...]).astype(BF16)
    gu = _dot(xn, wgu_ref[...])
    act = (_silu(gu[:, :D_FF]) * gu[:, D_FF:]).astype(BF16)
    h_ref[...] = h1 + _dot(act, wd_ref[...])


def _mix_ffn(x_r, x_sm, pool_r, pool_sm, attn_r, attn_sm, w_out, g, w_gu, w_down):
    r_spec = lambda w: pl.BlockSpec((TM, w), lambda i: (_r_idx(i), 0))
    sm_spec = lambda w: pl.BlockSpec((TM, w), lambda i: (_sm_idx(i), 0))
    return pl.pallas_call(
        _mix_ffn_kernel,
        grid=(F_TILES,),
        in_specs=[
            r_spec(D_MODEL), sm_spec(D_MODEL),
            r_spec(POOL_WIDTH), sm_spec(POOL_WIDTH),
            r_spec(ATTN_WIDTH), sm_spec(ATTN_WIDTH),
            _const_spec((D_MODEL, D_MODEL)),
            _const_spec((1, D_MODEL)),
            _const_spec((D_MODEL, 2 * D_FF)),
            _const_spec((D_FF, D_MODEL)),
        ],
        out_specs=pl.BlockSpec((TM, D_MODEL), lambda i: (i, 0)),
        out_shape=jax.ShapeDtypeStruct((NF, D_MODEL), F32),
        compiler_params=_cparams(("arbitrary",)),
        name="mix_ffn",
    )(x_r, x_sm, pool_r, pool_sm, attn_r, attn_sm, w_out, g, w_gu, w_down)


def _glu_kernel(h_ref, g_ref, w_ref, b_ref, u_ref):
    xn = _rms(h_ref[...], g_ref[...]).astype(BF16)
    a = _dot(xn, w_ref[...]) + b_ref[...]
    u_ref[...] = a[:, :CONV_CH] * jax.nn.sigmoid(a[:, CONV_CH:])


def _glu(h, g, w_pw1, b_pw1):
    return pl.pallas_call(
        _glu_kernel,
        grid=(F_TILES,),
        in_specs=[
            pl.BlockSpec((TM, D_MODEL), lambda i: (i, 0)),
            _const_spec((1, D_MODEL)),
            _const_spec((D_MODEL, 2 * CONV_CH)),
            _const_spec((1, 2 * CONV_CH)),
        ],
        out_specs=pl.BlockSpec((TM, CONV_CH), lambda i: (i, 0)),
        out_shape=jax.ShapeDtypeStruct((NF, CONV_CH), F32),
        compiler_params=_cparams(("arbitrary",)),
        name="glu",
    )(h, g, w_pw1, b_pw1)


CONV_HALO = 32


def _ln_swish_pw2(c, h, cg_ref, cb_ref, w2_ref, b2_ref):
    xc = c - jnp.mean(c, axis=-1, keepdims=True)
    var = jnp.mean(xc * xc, axis=-1, keepdims=True)
    n = xc * lax.rsqrt(var + EPS) * cg_ref[...] + cb_ref[...]
    return h + (_dot(_silu(n).astype(BF16), w2_ref[...]) + b2_ref[...])


def _conv_seq_kernel(u_ref, prev_ref, halo0_ref, h_ref, cw_ref, cbias_ref, cg_ref, cb_ref, w2_ref,
                     b2_ref, out_ref, z_ref, c_ref):
    i = pl.program_id(0)
    z_ref[0:CONV_HALO, :] = jnp.where(i % SEQ_TILES == 0, halo0_ref[...], prev_ref[...])
    z_ref[CONV_HALO:, :] = u_ref[...]
    off = CONV_HALO - (CONV_W - 1)
    sub = 8
    for cc in range(CONV_CH // LANES):
        sl = slice(cc * LANES, (cc + 1) * LANES)
        acc = None
        for ph in range(sub):
            rows = TM + (sub if ph else 0)
            part = None
            for a in range((CONV_W - 1 + off) // sub + 1):
                k = sub * a + ph - off
                if 0 <= k < CONV_W:
                    term = z_ref[sub * a:sub * a + rows, sl] * cw_ref[k:k + 1, sl]
                    part = term if part is None else part + term
            part = part[ph:ph + TM]
            acc = part if acc is None else acc + part
        c_ref[:, sl] = acc + cbias_ref[:, sl]
    out_ref[...] = _ln_swish_pw2(c_ref[...], h_ref[...], cg_ref, cb_ref, w2_ref, b2_ref)


def _conv_seq(u, halo0, h, conv_w, conv_b, cn_g, cn_b, w_pw2, b_pw2):
    per = TM // CONV_HALO
    vec = _const_spec((1, CONV_CH))
    return pl.pallas_call(
        _conv_seq_kernel,
        grid=(R_TILES,),
        in_specs=[
            pl.BlockSpec((TM, CONV_CH), lambda i: (i, 0)),
            pl.BlockSpec((CONV_HALO, CONV_CH), lambda i: (jnp.maximum(i * per - 1, 0), 0)),
            _const_spec((CONV_HALO, CONV_CH)),
            pl.BlockSpec((TM, D_MODEL), lambda i: (i, 0)),
            _const_spec((CONV_W, CONV_CH)),
            vec, vec, vec,
            _const_spec((CONV_CH, D_MODEL)),
            vec,
        ],
        out_specs=pl.BlockSpec((TM, D_MODEL), lambda i: (i, 0)),
        out_shape=jax.ShapeDtypeStruct((N_R, D_MODEL), F32),
        scratch_shapes=[pltpu.VMEM((CONV_HALO + TM, CONV_CH), F32), pltpu.VMEM((TM, CONV_CH), F32)],
        compiler_params=_cparams(("arbitrary",)),
        name="conv_seq",
    )(u, u, halo0, h, conv_w, conv_b, cn_g, cn_b, w_pw2, b_pw2)


def _conv_step_kernel(st_ref, u_ref, h_ref, cw_ref, cbias_ref, cg_ref, cb_ref, w2_ref, b2_ref, out_ref):
    n_st = CONV_W - 1
    for t in range(DEC_SEQ):
        acc = None
        for k in range(CONV_W):
            j = t + k
            zr = st_ref[j] if j < n_st else u_ref[j - n_st]
            term = zr * cw_ref[k:k + 1, :]
            acc = term if acc is None else acc + term
        out_ref[t] = _ln_swish_pw2(acc + cbias_ref[...], h_ref[t], cg_ref, cb_ref, w2_ref, b2_ref)


def _conv_step(state_t, u_t, h_t, conv_w, conv_b, cn_g, cn_b, w_pw2, b_pw2):
    return pl.pallas_call(
        _conv_step_kernel,
        out_shape=jax.ShapeDtypeStruct((DEC_SEQ, DEC_BATCH, D_MODEL), F32),
        compiler_params=pltpu.CompilerParams(vmem_limit_bytes=VMEM_LIMIT),
        name="conv_step",
    )(state_t, u_t, h_t, conv_w, conv_b, cn_g, cn_b, w_pw2, b_pw2)


TG = 512
NT_MAX = -(-(2 * N_RS + N_EXPERTS * (TG - 1)) // TG)
NS = NT_MAX * TG
REC_I1, REC_I2, REC_G1, REC_G2, REC_P1, REC_P2 = range(6)


def _lane_col(x, lane, k):
    return jnp.sum(jnp.where(lane == k, x, 0.0), axis=1, keepdims=True)


def _router_kernel(hr_ref, hs_ref, g_ref, wrh_ref, wrl_ref, rec_ref, cnt_ref, base_ref):
    i = pl.program_id(0)

    @pl.when(i == 0)
    def _():
        base_ref[...] = jnp.zeros_like(base_ref)

    h = jnp.where(i < R_TILES, hr_ref[...], hs_ref[...])
    xn = _rms(h, g_ref[...])
    xh = xn.astype(BF16)
    xl = (xn - xh.astype(F32)).astype(BF16)
    logits = _dot(xh, wrh_ref[...]) + (_dot(xl, wrh_ref[...]) + _dot(xh, wrl_ref[...]))
    lane = lax.broadcasted_iota(jnp.int32, (TM, LANES), 1)
    logits = jnp.where(lane < N_EXPERTS, logits, -jnp.inf)
    v1 = jnp.max(logits, axis=1, keepdims=True)
    i1 = jnp.min(jnp.where(logits == v1, lane, LANES), axis=1, keepdims=True)
    rest = jnp.where(lane == i1, -jnp.inf, logits)
    v2 = jnp.max(rest, axis=1, keepdims=True)
    i2 = jnp.min(jnp.where(rest == v2, lane, LANES), axis=1, keepdims=True)
    e2 = jnp.exp(v2 - v1)
    den = 1.0 + e2
    chosen = jnp.where((lane == i1) | (lane == i2), 1.0, 0.0)
    tri = jnp.where(lax.broadcasted_iota(jnp.int32, (TM, TM), 1) < lax.broadcasted_iota(jnp.int32, (TM, TM), 0),
                    1.0, 0.0).astype(BF16)
    rank = _dot(tri, chosen.astype(BF16)) + base_ref[...]
    base_ref[...] = base_ref[...] + jnp.sum(chosen, axis=0, keepdims=True)
    cnt_ref[...] = base_ref[...]
    rec = jnp.zeros((TM, LANES), F32)
    for k, val in ((REC_I1, i1.astype(F32)), (REC_I2, i2.astype(F32)), (REC_G1, 1.0 / den), (REC_G2, e2 / den),
                   (REC_P1, _lane_col(rank, lane, i1)), (REC_P2, _lane_col(rank, lane, i2))):
        rec = jnp.where(lane == k, val, rec)
    rec_ref[...] = rec


def _router(h_r, h_s, g, w_router_hi, w_router_lo):
    return pl.pallas_call(
        _router_kernel,
        grid=(RS_TILES,),
        in_specs=[
            pl.BlockSpec((TM, D_MODEL), lambda i: (_r_idx(i), 0)),
            pl.BlockSpec((TM, D_MODEL), lambda i: (_sm_idx(i), 0)),
            _const_spec((1, D_MODEL)),
            _const_spec((D_MODEL, LANES)),
            _const_spec((D_MODEL, LANES)),
        ],
        out_specs=[pl.BlockSpec((TM, LANES), lambda i: (i, 0)), _const_spec((1, LANES))],
        out_shape=[jax.ShapeDtypeStruct((N_RS, LANES), F32), jax.ShapeDtypeStruct((1, LANES), F32)],
        scratch_shapes=[pltpu.VMEM((1, LANES), F32)],
        compiler_params=_cparams(("arbitrary",)),
        name="router",
    )(h_r, h_s, g, w_router_hi, w_router_lo)


def _row_copy(src, s_row, dst, d_row, sem):
    return pltpu.make_async_copy(src.at[pl.ds(s_row, 1), :], dst.at[pl.ds(d_row, 1), :], sem)


def _wait_rows(src, dst, sem, n_bufs):
    for _ in range(n_bufs):
        pltpu.make_async_copy(src, dst, sem).wait()


def _dispatch_kernel(pad_ref, slots_ref, hr_ref, hs_ref, g_ref, xs_hbm, xbuf, zbuf, sem, zsem):
    i = pl.program_id(0)
    h = jnp.where(i < R_TILES, hr_ref[...], hs_ref[...])
    xbuf[...] = _rms(h, g_ref[...])

    def issue(t, carry):
        _row_copy(xbuf, t, xs_hbm, slots_ref[0, 0, 2 * t], sem).start()
        _row_copy(xbuf, t, xs_hbm, slots_ref[0, 0, 2 * t + 1], sem).start()
        return carry

    lax.fori_loop(0, TM, issue, 0, unroll=8)

    @pl.when(i == 0)
    def _():
        zbuf[...] = jnp.zeros_like(zbuf)
        for e in range(N_EXPERTS):
            def zero_row(r, carry, e=e):
                _row_copy(zbuf, 0, xs_hbm, pad_ref[0, e] + r, zsem).start()
                return carry
            lax.fori_loop(0, pad_ref[1, e], zero_row, 0)
        for e in range(N_EXPERTS):
            def wait_row(r, carry):
                _row_copy(zbuf, 0, xs_hbm, 0, zsem).wait()
                return carry
            lax.fori_loop(0, pad_ref[1, e], wait_row, 0)

        def tail_copy(j):
            return pltpu.make_async_copy(zbuf, xs_hbm.at[pl.ds(pl.multiple_of(j * TM, TM), TM), :], zsem)

        first_tail = pad_ref[2, 0] * (TG // TM)

        def zero_tail(j, carry):
            tail_copy(j).start()
            return carry

        def wait_tail(j, carry):
            tail_copy(j).wait()
            return carry

        lax.fori_loop(first_tail, NS // TM, zero_tail, 0)
        lax.fori_loop(first_tail, NS // TM, wait_tail, 0)

    _wait_rows(xbuf, xs_hbm.at[pl.ds(0, TM), :], sem, 2)


def _dispatch(pad_info, slots, h_r, h_s, g):
    grid_spec = pltpu.PrefetchScalarGridSpec(
        num_scalar_prefetch=1,
        grid=(RS_TILES,),
        in_specs=[
            pl.BlockSpec((1, 1, 2 * TM), lambda i, pad: (i, 0, 0), memory_space=pltpu.SMEM),
            pl.BlockSpec((TM, D_MODEL), lambda i, pad: (_r_idx(i), 0)),
            pl.BlockSpec((TM, D_MODEL), lambda i, pad: (_sm_idx(i), 0)),
            pl.BlockSpec((1, D_MODEL), lambda i, pad: (0, 0)),
        ],
        out_specs=pl.BlockSpec(memory_space=pl.ANY),
        scratch_shapes=[
            pltpu.VMEM((TM, D_MODEL), F32),
            pltpu.VMEM((TM, D_MODEL), F32),
            pltpu.SemaphoreType.DMA(()),
            pltpu.SemaphoreType.DMA(()),
        ],
    )
    return pl.pallas_call(
        _dispatch_kernel,
        grid_spec=grid_spec,
        out_shape=jax.ShapeDtypeStruct((NS, D_MODEL), F32),
        compiler_params=_cparams(("arbitrary",)),
        name="dispatch",
    )(pad_info, slots, h_r, h_s, g)


def _experts_kernel(te_ref, nu_ref, x_ref, wgu_ref, wd_ref, y_ref):
    j = pl.program_id(0)

    @pl.when(j < nu_ref[0])
    def _():
        gu = _dot(x_ref[...].astype(BF16), wgu_ref[0])
        act = (_silu(gu[:, :EXPERT_FF]) * gu[:, EXPERT_FF:]).astype(BF16)
        y_ref[...] = _dot(act, wd_ref[0])

    @pl.when(j >= nu_ref[0])
    def _():
        y_ref[...] = jnp.zeros_like(y_ref)


def _experts(tile_expert, n_used, xs, w_gu_x, w_down_x):
    grid_spec = pltpu.PrefetchScalarGridSpec(
        num_scalar_prefetch=2,
        grid=(NT_MAX,),
        in_specs=[
            pl.BlockSpec((TG, D_MODEL), lambda j, te, nu: (jnp.minimum(j, nu[0] - 1), 0)),
            pl.BlockSpec((1, D_MODEL, 2 * EXPERT_FF), lambda j, te, nu: (te[j], 0, 0)),
            pl.BlockSpec((1, EXPERT_FF, D_MODEL), lambda j, te, nu: (te[j], 0, 0)),
        ],
        out_specs=pl.BlockSpec((TG, D_MODEL), lambda j, te, nu: (j, 0)),
    )
    return pl.pallas_call(
        _experts_kernel,
        grid_spec=grid_spec,
        out_shape=jax.ShapeDtypeStruct((NS, D_MODEL), F32),
        compiler_params=_cparams(("arbitrary",)),
        name="experts",
    )(tile_expert, n_used, xs, w_gu_x, w_down_x)


def _combine_kernel(slots_ref, next_slots_ref, hr_ref, hs_ref, rec_ref, fg_ref, ys_hbm, outr_ref, outs_ref,
                    r1buf, r2buf, sems):
    i = pl.program_id(0)
    half = i % 2

    def gather(src_slots, sl):
        def issue(t, carry):
            _row_copy(ys_hbm, src_slots[0, 0, 2 * t], r1buf.at[sl], t, sems.at[sl]).start()
            _row_copy(ys_hbm, src_slots[0, 0, 2 * t + 1], r2buf.at[sl], t, sems.at[sl]).start()
            return carry
        lax.fori_loop(0, TM, issue, 0, unroll=8)

    @pl.when(i == 0)
    def _():
        gather(slots_ref, 0)

    @pl.when(i + 1 < pl.num_programs(0))
    def _():
        gather(next_slots_ref, 1 - half)

    h = jnp.where(i < R_TILES, hr_ref[...], hs_ref[...])
    lane = lax.broadcasted_iota(jnp.int32, (TM, LANES), 1)
    rec = rec_ref[...]
    g1 = _lane_col(rec, lane, REC_G1)
    g2 = _lane_col(rec, lane, REC_G2)
    _wait_rows(ys_hbm.at[pl.ds(0, TM), :], r1buf.at[half], sems.at[half], 2)
    res = _rms(h + g1 * r1buf[half] + g2 * r2buf[half], fg_ref[...])

    @pl.when(i < R_TILES)
    def _():
        outr_ref[...] = res

    @pl.when(i >= R_TILES)
    def _():
        outs_ref[...] = res


def _combine(slots, h_r, h_s, rec, fg, ys):
    return pl.pallas_call(
        _combine_kernel,
        grid=(RS_TILES,),
        in_specs=[
            pl.BlockSpec((1, 1, 2 * TM), lambda i: (i, 0, 0), memory_space=pltpu.SMEM),
            pl.BlockSpec((1, 1, 2 * TM), lambda i: (jnp.minimum(i + 1, RS_TILES - 1), 0, 0),
                         memory_space=pltpu.SMEM),
            pl.BlockSpec((TM, D_MODEL), lambda i: (_r_idx(i), 0)),
            pl.BlockSpec((TM, D_MODEL), lambda i: (_sm_idx(i), 0)),
            pl.BlockSpec((TM, LANES), lambda i: (i, 0)),
            _const_spec((1, D_MODEL)),
            pl.BlockSpec(memory_space=pl.ANY),
        ],
        out_specs=[
            pl.BlockSpec((TM, D_MODEL), lambda i: (_r_idx(i), 0)),
            pl.BlockSpec((TM, D_MODEL), lambda i: (_sm_idx(i), 0)),
        ],
        out_shape=[jax.ShapeDtypeStruct((N_R, D_MODEL), F32), jax.ShapeDtypeStruct((N_S, D_MODEL), F32)],
        scratch_shapes=[
            pltpu.VMEM((2, TM, D_MODEL), F32),
            pltpu.VMEM((2, TM, D_MODEL), F32),
            pltpu.SemaphoreType.DMA((2,)),
        ],
        compiler_params=_cparams(("arbitrary",)),
        name="combine",
    )(slots, slots, h_r, h_s, rec, fg, ys)


def _moe(h_r, h_s, g, w_router_hi, w_router_lo, w_gu_x, w_down_x, fg):
    rec, counts = _router(h_r, h_s, g, w_router_hi, w_router_lo)
    cnt = counts[0, :N_EXPERTS].astype(jnp.int32)
    tiles = (cnt + (TG - 1)) // TG
    eidx = jnp.arange(N_EXPERTS, dtype=jnp.int32)
    tile_end = jnp.sum(jnp.where(eidx[None, :] <= eidx[:, None], tiles[None, :], 0), axis=1)
    slot_off = (tile_end - tiles) * TG
    ids = rec[:, REC_I1:REC_I2 + 1].astype(jnp.int32)
    pos = rec[:, REC_P1:REC_P2 + 1].astype(jnp.int32)
    tok_off = jnp.sum(jnp.where(ids[..., None] == eidx, slot_off, 0), axis=-1)
    slots = (tok_off + pos).reshape(RS_TILES, 1, 2 * TM)
    tile_idx = jnp.arange(NT_MAX, dtype=jnp.int32)
    tile_expert = jnp.minimum(jnp.sum((tile_idx[:, None] >= tile_end[None, :]).astype(jnp.int32), axis=1),
                              N_EXPERTS - 1)
    n_used = tile_end[-1:]
    pad_info = jnp.stack([slot_off + cnt, tiles * TG - cnt, jnp.broadcast_to(n_used, (N_EXPERTS,))])
    xs = _dispatch(pad_info, slots, h_r, h_s, g)
    ys = _experts(tile_expert, n_used, xs, w_gu_x, w_down_x)
    return _combine(slots, h_r, h_s, rec, fg, ys)


def _rope_tables(past_len):
    pos = jnp.concatenate([
        jnp.arange(SEQ, dtype=F32) + float(N_META),
        jnp.tile(jnp.arange(DEC_SEQ, dtype=F32) + float(past_len), DEC_BATCH),
        jnp.arange(N_META, dtype=F32),
        jnp.zeros((TM - N_META,), F32),
    ])
    inv = ROPE_THETA ** (-jnp.arange(0, QK_ROPE, 2, dtype=F32) / QK_ROPE)
    ang = pos[:, None] * inv[None, :]
    cos, sin = jnp.cos(ang), jnp.sin(ang)
    rest = (pos.shape[0], LANES - QK_ROPE)
    return (jnp.concatenate([cos, cos, jnp.ones(rest, F32)], axis=1),
            jnp.concatenate([-sin, sin, jnp.zeros(rest, F32)], axis=1))


def kernel(x_prompt, x_sample, cache_mla, page_table, state_pool, state_conv, meta_tokens, norm_mix_e, w_in_e, pool_w, pool_scale, q_norm, kv_norm, w_uq, w_uk, w_uv, w_out_e, norm_ffn_e, w_gu_e, w_down_e, norm_mix_o, w_pw1, b_pw1, conv_w, conv_b, cnorm_g, cnorm_b, w_pw2, b_pw2, norm_ffn_o, router_w, w_gu_x, w_down_x, final_norm):
    assert x_prompt.shape == (BATCH, SEQ, D_MODEL) and x_sample.shape == (DEC_BATCH, DEC_SEQ, D_MODEL)
    assert norm_mix_e.shape[0] == 1 and norm_mix_o.shape[0] == 1
    past_len = page_table.shape[1] * PAGE_SIZE
    vec = lambda v: v.reshape(1, -1).astype(F32)

    w_in_p = jnp.pad(w_in_e[0], ((0, 0), (0, IN_W_PAD - w_in_e.shape[2]))).astype(BF16)
    uq = w_uq[0].reshape(Q_LORA, N_HEADS, QK_NOPE + QK_ROPE)
    w_uq_p = jnp.concatenate(
        [uq[..., QK_NOPE:], uq[..., :QK_NOPE], jnp.zeros((Q_LORA, N_HEADS, LANES - QK_NOPE - QK_ROPE), F32)],
        axis=-1).reshape(Q_LORA, N_HEADS * LANES).astype(BF16)
    ukt = jnp.transpose(w_uk[0], (1, 2, 0))
    w_ukt_p = jnp.pad(ukt, ((0, 0), (QK_ROPE, LANES - QK_NOPE - QK_ROPE), (0, 0))).astype(BF16)
    w_uv_h = jnp.transpose(w_uv[0], (1, 0, 2)).astype(BF16)
    hw = N_HEADS * LANES
    w_uk_p = jnp.pad(w_uk[0], ((0, 0), (0, 0), (QK_ROPE, LANES - QK_NOPE - QK_ROPE))).reshape(KV_LORA, hw).astype(BF16)
    w_uv_p = jnp.pad(w_uv[0], ((0, 0), (0, 0), (0, LANES - V_HEAD))).reshape(KV_LORA, hw).astype(BF16)
    pool_w_b = pool_w[0].astype(BF16)
    w_router_p = jnp.pad(router_w[0], ((0, 0), (0, LANES - N_EXPERTS))).astype(F32)
    w_router_hi = w_router_p.astype(BF16)
    w_router_lo = (w_router_p - w_router_hi.astype(F32)).astype(BF16)
    cos_t, sin_t = _rope_tables(past_len)

    x_r = x_prompt.reshape(N_R, D_MODEL)
    x_sm = jnp.concatenate([x_sample.reshape(N_S, D_MODEL), meta_tokens.astype(F32),
                            jnp.zeros((TM - N_META, D_MODEL), F32)], axis=0)

    upool, q_all, k_all, v_all, newkv, qcat_sm = _inproj(
        x_r, x_sm, vec(norm_mix_e[0]), w_in_p, vec(q_norm[0]), vec(kv_norm[0]), w_uq_p, w_uk_p, w_uv_p,
        w_ukt_p, cos_t, sin_t)
    pscale = vec(pool_scale[0])
    pool_r = _pool_seq(upool, upool, pool_w_b, pscale, n_seq=BATCH, seq_len=SEQ, first_block=0,
                       halo_block=META_ROW // POOL_HALO, pos0=N_META)
    pool_m = _pool_seq(upool, jnp.zeros((POOL_HALO, POOL_WIDTH), F32), pool_w_b, pscale, n_seq=1,
                       seq_len=N_META, first_block=META_ROW // N_META, halo_block=0, pos0=0)
    upool_s = upool[N_R:N_RS].reshape(DEC_BATCH, DEC_SEQ, POOL_WIDTH)
    pool_s = _pool_step(jnp.transpose(state_pool[0], (1, 0, 2)), jnp.transpose(upool_s, (1, 0, 2)),
                        pool_w_b, pscale)
    pool_s = jnp.transpose(pool_s, (1, 0, 2)).reshape(N_S, POOL_WIDTH)
    pool_sm = jnp.concatenate([pool_s, pool_m, jnp.zeros((TM - N_META, POOL_WIDTH), BF16)], axis=0)

    meta_pad = ((0, LANES - N_META), (0, 0))
    k_meta = jnp.pad(k_all[META_ROW:META_ROW + N_META], meta_pad)
    v_meta = jnp.pad(v_all[META_ROW:META_ROW + N_META], meta_pad)
    attn_r = _attn_prompt(q_all, k_all, v_all, k_meta, v_meta)
    attn_m = _attn_meta(q_all, k_meta, v_meta)
    q_s = qcat_sm[:, :N_S].reshape(N_HEADS, DEC_BATCH, DEC_SEQ, KCAT)
    q_s = jnp.transpose(q_s, (1, 0, 2, 3)).reshape(DEC_BATCH, DEC_ROWS, KCAT)
    k_new = jnp.pad(newkv[N_R:N_RS].reshape(DEC_BATCH, DEC_SEQ, KV_LORA + QK_ROPE),
                    ((0, 0), (0, DEC_NEW_PAD - DEC_SEQ), (0, KCAT - KV_LORA - QK_ROPE))).astype(BF16)
    o_lat = _attn_step(page_table.astype(jnp.int32), q_s, k_new, jnp.transpose(cache_mla, (0, 1, 3, 2)))
    o_lat = jnp.transpose(o_lat.reshape(DEC_BATCH, N_HEADS, DEC_SEQ, KV_LORA), (0, 2, 1, 3))
    attn_s = _uv_proj(o_lat.reshape(N_S, N_HEADS * KV_LORA), w_uv_h)
    attn_sm = jnp.concatenate([attn_s, attn_m, jnp.zeros((TM - N_META, ATTN_WIDTH), BF16)], axis=0)

    h2 = _mix_ffn(x_r, x_sm, pool_r, pool_sm, attn_r, attn_sm, w_out_e[0].astype(BF16),
                  vec(norm_ffn_e[0]), w_gu_e[0].astype(BF16), w_down_e[0].astype(BF16))

    u = _glu(h2, vec(norm_mix_o[0]), w_pw1[0].astype(BF16), vec(b_pw1[0]))
    conv_args = (conv_w[0].astype(F32), vec(conv_b[0]), vec(cnorm_g[0]), vec(cnorm_b[0]),
                 w_pw2[0].astype(BF16), vec(b_pw2[0]))
    halo0 = jnp.concatenate([jnp.zeros((CONV_HALO - N_META, CONV_CH), F32),
                             u[META_ROW:META_ROW + N_META]], axis=0)
    h3_r = _conv_seq(u, halo0, h2, *conv_args)
    u_s = u[N_R:N_RS].reshape(DEC_BATCH, DEC_SEQ, CONV_CH)
    h2_s = h2[N_R:N_RS].reshape(DEC_BATCH, DEC_SEQ, D_MODEL)
    h3_s = _conv_step(jnp.transpose(state_conv[0], (1, 0, 2)), jnp.transpose(u_s, (1, 0, 2)),
                      jnp.transpose(h2_s, (1, 0, 2)), *conv_args)
    h3_s = jnp.transpose(h3_s, (1, 0, 2)).reshape(N_S, D_MODEL)

    y_r, y_s = _moe(h3_r, h3_s, vec(norm_ffn_o[0]), w_router_hi, w_router_lo, w_gu_x[0].astype(BF16),
                    w_down_x[0].astype(BF16), vec(final_norm))
    y_prompt = y_r.reshape(BATCH, SEQ, D_MODEL)
    y_sample = y_s.reshape(DEC_BATCH, DEC_SEQ, D_MODEL)

    kv_w = KV_LORA + QK_ROPE
    kv_meta = jnp.broadcast_to(newkv[META_ROW:META_ROW + N_META][None], (BATCH, N_META, kv_w))
    new_mla_prompt = jnp.concatenate([kv_meta, newkv[:N_R].reshape(BATCH, SEQ, kv_w)], axis=1)[None]
    new_mla_sample = newkv[N_R:N_RS].reshape(1, DEC_BATCH, DEC_SEQ, kv_w)
    new_pool_prompt = jnp.stack([upool[(b + 1) * SEQ - POOL_BUF:(b + 1) * SEQ] for b in range(BATCH)])[None]
    new_pool_sample = jnp.concatenate([state_pool[0][:, DEC_SEQ:], upool_s], axis=1)[None]
    new_conv_prompt = jnp.stack([u[(b + 1) * SEQ - (CONV_W - 1):(b + 1) * SEQ] for b in range(BATCH)])[None]
    new_conv_sample = jnp.concatenate([state_conv[0][:, DEC_SEQ:].astype(F32), u_s], axis=1)[None]
    return (y_prompt, y_sample, new_mla_prompt, new_mla_sample, new_pool_prompt, new_pool_sample,
            new_conv_prompt, new_conv_sample)
```
